```python
import math
import jax, jax.numpy as jnp
from jax import lax
import numpy as np

D_MODEL = 1024
BATCH = 4
SEQ = 8192
DEPTH = 2
DEC_BATCH = 128
DEC_SEQ = 1
PAST_LEN = 16384
PAGE_SIZE = 128

F32 = jnp.float32
HEAD_DIM = 64
ATTN_SCALE = HEAD_DIM ** -0.5
A_HEADS = 8
A_KV_HEADS = 2
A_REP = A_HEADS // A_KV_HEADS
A_WIDTH = A_HEADS * HEAD_DIM
A_KV_WIDTH = A_KV_HEADS * HEAD_DIM
WINDOW = 128
ROPE_THETA = 10000.0
B_HEADS = 8
B_HEAD_DIM = 64
B_WIDTH = B_HEADS * B_HEAD_DIM
B_GROUPS = 2
B_REP = B_HEADS // B_GROUPS
D_STATE = 128
CONV_WIDTH = 4
CONV_DIM = B_WIDTH + 2 * B_GROUPS * D_STATE
SSD_CHUNK = 128
C_HEADS = 16
C_KV_HEADS = 4
C_REP = C_HEADS // C_KV_HEADS
C_WIDTH = C_HEADS * HEAD_DIM
C_KV_WIDTH = C_KV_HEADS * HEAD_DIM
Q_BLOCK = 128
N_EVEN = (DEPTH + 1) // 2
N_ODD = DEPTH // 2
DN_ALPHA = (2 * DEPTH) ** 0.25
DN_BETA = (8 * DEPTH) ** -0.25
LN_EPS = 1e-5
RMS_EPS = 1e-5
EVEN_SPLITS = (A_WIDTH, A_KV_WIDTH, A_KV_WIDTH, A_WIDTH, B_WIDTH, CONV_DIM, B_HEADS)
EVEN_IN = A_WIDTH + 2 * A_KV_WIDTH + A_WIDTH + B_WIDTH + CONV_DIM + B_HEADS
ODD_SPLITS = (C_WIDTH, C_KV_WIDTH, C_KV_WIDTH, C_HEADS, C_WIDTH)
ODD_IN = 2 * C_WIDTH + 2 * C_KV_WIDTH + C_HEADS

kernel_name = 'hybrid_swa_ssd_fox_step'


def _split(h, sizes):
    offs = [int(o) for o in np.cumsum(sizes)[:-1]]
    return jnp.split(h, offs, axis=-1)


def _layer_norm(x, g, b):
    xf = x.astype(F32)
    mu = jnp.mean(xf, axis=-1, keepdims=True)
    var = jnp.mean(jnp.square(xf - mu), axis=-1, keepdims=True)
    return ((xf - mu) * lax.rsqrt(var + LN_EPS) * g.astype(F32) + b.astype(F32)).astype(x.dtype)


def _rope(x, pos):
    half = HEAD_DIM // 2
    inv_freq = ROPE_THETA ** (-jnp.arange(half, dtype=F32) / half)
    ang = pos.astype(F32)[:, None] * inv_freq[None, :]
    cos = jnp.cos(ang)[None, :, None, :]
    sin = jnp.sin(ang)[None, :, None, :]
    xf = x.astype(F32)
    x1, x2 = xf[..., :half], xf[..., half:]
    return jnp.concatenate([x1 * cos - x2 * sin, x2 * cos + x1 * sin], axis=-1).astype(x.dtype)


def _sink_softmax(s, sink):
    m = jnp.maximum(jnp.max(s, axis=-1, keepdims=True), sink)
    p = jnp.exp(s - m)
    return p / (jnp.sum(p, axis=-1, keepdims=True) + jnp.exp(sink - m))


def _swa_prompt(q, k, v, sinks):
    b, t = q.shape[:2]
    nb = t // WINDOW
    qb = q.reshape(b, nb, WINDOW, A_KV_HEADS, A_REP, HEAD_DIM)
    kb = k.reshape(b, nb, WINDOW, A_KV_HEADS, HEAD_DIM)
    vb = v.reshape(b, nb, WINDOW, A_KV_HEADS, HEAD_DIM)
    pad = ((0, 0), (1, 0), (0, 0), (0, 0), (0, 0))
    kk = jnp.concatenate([jnp.pad(kb, pad)[:, :-1], kb], axis=2)
    vv = jnp.concatenate([jnp.pad(vb, pad)[:, :-1], vb], axis=2)
    s = jnp.einsum('bnqkrd,bnskd->bnkrqs', qb, kk, preferred_element_type=F32) * ATTN_SCALE
    qpos = WINDOW + jnp.arange(WINDOW)
    kpos = jnp.arange(2 * WINDOW)
    diff = qpos[:, None] - kpos[None, :]
    band = (diff >= 0) & (diff <= WINDOW)
    key_ok = (kpos[None, :] >= WINDOW) | (jnp.arange(nb)[:, None] > 0)
    mask = band[None] & key_ok[:, None, :]
    s = jnp.where(mask[None, :, None, None], s, -jnp.inf)
    sink = sinks.astype(F32).reshape(A_KV_HEADS, A_REP)[:, :, None, None]
    p = _sink_softmax(s, sink).astype(v.dtype)
    o = jnp.einsum('bnkrqs,bnskd->bnqkrd', p, vv)
    return o.reshape(b, t, A_WIDTH)


def _swa_sample(q, k, v, buf_k, buf_v, sinks):
    b, t = q.shape[:2]
    kk = jnp.concatenate([buf_k.astype(k.dtype), k], axis=1)
    vv = jnp.concatenate([buf_v.astype(v.dtype), v], axis=1)
    qq = q.reshape(b, t, A_KV_HEADS, A_REP, HEAD_DIM)
    s = jnp.einsum('bqkrd,bskd->bkrqs', qq, kk, preferred_element_type=F32) * ATTN_SCALE
    qpos = WINDOW + jnp.arange(t)
    kpos = jnp.arange(WINDOW + t)
    diff = qpos[:, None] - kpos[None, :]
    s = jnp.where((diff >= 0) & (diff <= WINDOW), s, -jnp.inf)
    sink = sinks.astype(F32).reshape(A_KV_HEADS, A_REP)[:, :, None, None]
    p = _sink_softmax(s, sink).astype(v.dtype)
    o = jnp.einsum('bkrqs,bskd->bqkrd', p, vv).reshape(b, t, A_WIDTH)
    return o, kk[:, -WINDOW:], vv[:, -WINDOW:]


def _causal_conv(xbc, hist, w, bias):
    t = xbc.shape[1]
    xp = jnp.concatenate([hist.astype(xbc.dtype), xbc], axis=1)
    y = bias + sum(xp[:, j:j + t] * w[j] for j in range(CONV_WIDTH))
    return jax.nn.silu(y), xp[:, -(CONV_WIDTH - 1):]


def _ssd(x, dt, a, bm, cm, h0, chunk):
    b, t = x.shape[:2]
    nc = t // chunk
    shp = (b, nc, chunk, B_GROUPS, B_REP)
    xdt = (x.astype(F32) * dt[..., None]).reshape(*shp, B_HEAD_DIM)
    cum = jnp.cumsum((dt * a).reshape(shp), axis=2)
    bm = bm.astype(F32).reshape(b, nc, chunk, B_GROUPS, D_STATE)
    cm = cm.astype(F32).reshape(b, nc, chunk, B_GROUPS, D_STATE)
    causal = jnp.tril(jnp.ones((chunk, chunk), bool))[None, None, :, :, None, None]
    seg = cum[:, :, :, None] - cum[:, :, None, :]
    decay = jnp.exp(jnp.where(causal, seg, -jnp.inf))
    cb = jnp.einsum('bclgn,bcsgn->bclsg', cm, bm)
    y_diag = jnp.einsum('bclsg,bclsgr,bcsgrp->bclgrp', cb, decay, xdt)
    end_decay = jnp.exp(cum[:, :, -1:] - cum)
    chunk_states = jnp.einsum('bclgn,bclgr,bclgrp->bcgrpn', bm, end_decay, xdt)
    chunk_decay = jnp.exp(cum[:, :, -1])

    def step(h, inp):
        st, dec = inp
        return h * dec[..., None, None] + st, h

    h_init = h0.astype(F32).reshape(b, B_GROUPS, B_REP, B_HEAD_DIM, D_STATE)
    h_last, h_in = lax.scan(step, h_init, (jnp.moveaxis(chunk_states, 1, 0), jnp.moveaxis(chunk_decay, 1, 0)))
    h_in = jnp.moveaxis(h_in, 0, 1)
    y_off = jnp.einsum('bclgn,bcgrpn,bclgr->bclgrp', cm, h_in, jnp.exp(cum))
    y = (y_diag + y_off).reshape(b, t, B_HEADS, B_HEAD_DIM)
    return y, h_last.reshape(b, B_HEADS, B_HEAD_DIM, D_STATE)


def _gated_rmsnorm(y, z, w):
    g = y.astype(F32) * jax.nn.silu(z.astype(F32))
    g = g.reshape(*g.shape[:-1], B_GROUPS, B_WIDTH // B_GROUPS)
    g = g * lax.rsqrt(jnp.mean(g * g, axis=-1, keepdims=True) + RMS_EPS)
    return (g.reshape(y.shape) * w.astype(F32)).astype(z.dtype)


def _even_layer(x, pos, sample, swa_k, swa_v, conv_hist, ssm_h0,
                w_in, sinks, conv_w, conv_b, dt_bias, a_log, d_skip, norm_w, w_out, ln_g, ln_b):
    b, t, _ = x.shape
    h = jnp.einsum('btd,de->bte', x, w_in)
    q, k, v, g_a, z, xbc, dt_raw = _split(h, EVEN_SPLITS)
    q = _rope(q.reshape(b, t, A_HEADS, HEAD_DIM), pos)
    k = _rope(k.reshape(b, t, A_KV_HEADS, HEAD_DIM), pos)
    v = v.reshape(b, t, A_KV_HEADS, HEAD_DIM)
    if sample:
        o_a, new_k, new_v = _swa_sample(q, k, v, swa_k, swa_v, sinks)
        hist, h0, chunk = conv_hist, ssm_h0, t
    else:
        o_a = _swa_prompt(q, k, v, sinks)
        new_k, new_v = k[:, -WINDOW:], v[:, -WINDOW:]
        hist = jnp.zeros((b, CONV_WIDTH - 1, CONV_DIM), x.dtype)
        h0 = jnp.zeros((b, B_HEADS, B_HEAD_DIM, D_STATE), F32)
        chunk = SSD_CHUNK
    o_a = o_a * jax.nn.silu(g_a)
    xbc, new_conv = _causal_conv(xbc, hist, conv_w, conv_b)
    xs, bm, cm = _split(xbc, (B_WIDTH, B_GROUPS * D_STATE, B_GROUPS * D_STATE))
    xs = xs.reshape(b, t, B_HEADS, B_HEAD_DIM)
    dt = jax.nn.softplus(dt_raw.astype(F32) + dt_bias.astype(F32))
    a = -jnp.exp(a_log.astype(F32))
    y, h_new = _ssd(xs, dt, a, bm.reshape(b, t, B_GROUPS, D_STATE), cm.reshape(b, t, B_GROUPS, D_STATE), h0, chunk)
    y = y + d_skip.astype(F32)[:, None] * xs.astype(F32)
    o_b = _gated_rmsnorm(y.reshape(b, t, B_WIDTH), z, norm_w)
    mix = jnp.einsum('bte,ed->btd', jnp.concatenate([o_a, o_b], axis=-1), w_out)
    x_new = _layer_norm(DN_ALPHA * x + mix, ln_g, ln_b)
    return x_new, new_k, new_v, new_conv, h_new.astype(x.dtype)


def _fox_prompt(q, k, v, logf):
    b, t = q.shape[:2]
    nb = t // Q_BLOCK
    c = jnp.cumsum(logf, axis=1).reshape(b, t, C_KV_HEADS, C_REP)
    c_key = jnp.transpose(c, (0, 2, 3, 1))[:, :, :, None, :]
    q_blocks = jnp.moveaxis(q.reshape(b, nb, Q_BLOCK, C_KV_HEADS, C_REP, HEAD_DIM), 1, 0)
    c_blocks = jnp.moveaxis(c.reshape(b, nb, Q_BLOCK, C_KV_HEADS, C_REP), 1, 0)
    starts = jnp.arange(nb) * Q_BLOCK
    kpos = jnp.arange(t)

    def block(args):
        qb, cq, st = args
        s = jnp.einsum('bqkrd,bskd->bkrqs', qb, k, preferred_element_type=F32) * ATTN_SCALE
        s = s + jnp.transpose(cq, (0, 2, 3, 1))[..., None] - c_key
        qpos = st + jnp.arange(Q_BLOCK)
        s = jnp.where(kpos[None, :] <= qpos[:, None], s, -jnp.inf)
        p = jax.nn.softmax(s, axis=-1).astype(v.dtype)
        return jnp.einsum('bkrqs,bskd->bqkrd', p, v)

    o = lax.map(block, (q_blocks, c_blocks, starts))
    return jnp.moveaxis(o, 0, 1).reshape(b, t, C_WIDTH)


def _fox_sample(q, k, v, logf, past_k, past_v, past_lf):
    b, t = q.shape[:2]
    past_k = past_k.reshape(b, -1, C_KV_HEADS, HEAD_DIM)
    past_v = past_v.reshape(b, -1, C_KV_HEADS, HEAD_DIM)
    n_past = past_k.shape[1]
    lf_all = jnp.concatenate([past_lf.reshape(b, n_past, C_HEADS).astype(F32), logf], axis=1)
    c = jnp.cumsum(lf_all, axis=1).reshape(b, n_past + t, C_KV_HEADS, C_REP)
    c_key = jnp.transpose(c, (0, 2, 3, 1))[:, :, :, None, :]
    c_q = jnp.transpose(c[:, n_past:], (0, 2, 3, 1))[..., None]
    s = jnp.concatenate([
        jnp.einsum('bqkrd,bskd->bkrqs', q, past_k, preferred_element_type=F32),
        jnp.einsum('bqkrd,bskd->bkrqs', q, k, preferred_element_type=F32)], axis=-1) * ATTN_SCALE
    s = s + c_q - c_key
    qpos = n_past + jnp.arange(t)
    kpos = jnp.arange(n_past + t)
    s = jnp.where(kpos[None, :] <= qpos[:, None], s, -jnp.inf)
    p = jax.nn.softmax(s, axis=-1).astype(v.dtype)
    return (jnp.einsum('bkrqs,bskd->bqkrd', p[..., :n_past], past_v.astype(v.dtype))
            + jnp.einsum('bkrqs,bskd->bqkrd', p[..., n_past:], v))


def _odd_layer(x, sample, past_k, past_v, past_lf, w_in, f_bias, w_out, ln_g, ln_b):
    b, t, _ = x.shape
    h = jnp.einsum('btd,de->bte', x, w_in)
    q, k, v, f_raw, g_c = _split(h, ODD_SPLITS)
    q = q.reshape(b, t, C_KV_HEADS, C_REP, HEAD_DIM)
    k = k.reshape(b, t, C_KV_HEADS, HEAD_DIM)
    v = v.reshape(b, t, C_KV_HEADS, HEAD_DIM)
    logf = jax.nn.log_sigmoid(f_raw.astype(F32) + f_bias.astype(F32))
    if sample:
        o = _fox_sample(q, k, v, logf, past_k, past_v, past_lf)
    else:
        o = _fox_prompt(q, k, v, logf)
    o = o.reshape(b, t, C_WIDTH) * jax.nn.silu(g_c)
    mix = jnp.einsum('bte,ed->btd', o, w_out)
    x_new = _layer_norm(DN_ALPHA * x + mix, ln_g, ln_b)
    return x_new, k, v, logf


def setup_inputs(seed: int = 0) -> dict:
    key = jax.random.key(seed)
    ks = jax.random.split(key, 26)
    n_pages = PAST_LEN // PAGE_SIZE
    n_phys = (DEC_BATCH * n_pages * 5) // 4
    x_prompt = jax.random.normal(ks[0], (BATCH, SEQ, D_MODEL), F32)
    x_sample = jax.random.normal(ks[1], (DEC_BATCH, DEC_SEQ, D_MODEL), F32)
    cache_swa_k = jax.random.normal(ks[2], (N_EVEN, DEC_BATCH, WINDOW, A_KV_HEADS, HEAD_DIM), F32)
    cache_swa_v = jax.random.normal(ks[3], (N_EVEN, DEC_BATCH, WINDOW, A_KV_HEADS, HEAD_DIM), F32)
    state_conv = jax.random.normal(ks[4], (N_EVEN, DEC_BATCH, CONV_WIDTH - 1, CONV_DIM), F32)
    state_ssm = 0.1 * jax.random.normal(ks[5], (N_EVEN, DEC_BATCH, B_HEADS, B_HEAD_DIM, D_STATE), F32)
    cache_fox_k = jax.random.normal(ks[6], (N_ODD, n_phys, PAGE_SIZE, C_KV_HEADS, HEAD_DIM), F32)
    cache_fox_v = jax.random.normal(ks[7], (N_ODD, n_phys, PAGE_SIZE, C_KV_HEADS, HEAD_DIM), F32)
    cache_fox_logf = jax.nn.log_sigmoid(3.5 + jax.random.normal(ks[8], (N_ODD, n_phys, PAGE_SIZE, C_HEADS), F32))
    page_table = jax.random.permutation(ks[9], n_phys)[:DEC_BATCH * n_pages].reshape(DEC_BATCH, n_pages).astype(jnp.int32)

    even_cols = jnp.concatenate([jnp.full((s,), DN_BETA if i == 2 else 1.0, F32) for i, s in enumerate(EVEN_SPLITS)])
    odd_cols = jnp.concatenate([jnp.full((s,), DN_BETA if i == 2 else 1.0, F32) for i, s in enumerate(ODD_SPLITS)])
    w_in_even = jax.random.normal(ks[10], (N_EVEN, D_MODEL, EVEN_IN), F32) * (D_MODEL ** -0.5) * even_cols
    attn_sinks = 0.5 * jax.random.normal(ks[11], (N_EVEN, A_HEADS), F32)
    conv_w = jax.random.normal(ks[12], (N_EVEN, CONV_WIDTH, CONV_DIM), F32) * (CONV_WIDTH ** -0.5)
    conv_b = 0.02 * jax.random.normal(ks[13], (N_EVEN, CONV_DIM), F32)
    dt0 = jnp.exp(jax.random.uniform(ks[14], (N_EVEN, B_HEADS), F32, math.log(1e-3), math.log(1e-1)))
    dt_bias = dt0 + jnp.log(-jnp.expm1(-dt0))
    a_log = jnp.log(jax.random.uniform(ks[15], (N_EVEN, B_HEADS), F32, 1.0, 16.0))
    d_skip = 1.0 + 0.1 * jax.random.normal(ks[16], (N_EVEN, B_HEADS), F32)
    ssm_norm_w = 1.0 + 0.02 * jax.random.normal(ks[17], (N_EVEN, B_WIDTH), F32)
    w_out_even = jax.random.normal(ks[18], (N_EVEN, A_WIDTH + B_WIDTH, D_MODEL), F32) * ((A_WIDTH + B_WIDTH) ** -0.5) * DN_BETA
    ln_g_even = 1.0 + 0.02 * jax.random.normal(ks[19], (N_EVEN, D_MODEL), F32)
    ln_b_even = 0.02 * jax.random.normal(ks[20], (N_EVEN, D_MODEL), F32)
    w_in_odd = jax.random.normal(ks[21], (N_ODD, D_MODEL, ODD_IN), F32) * (D_MODEL ** -0.5) * odd_cols
    forget_bias = jax.random.uniform(ks[22], (N_ODD, C_HEADS), F32, 1.0, 6.0)
    w_out_odd = jax.random.normal(ks[23], (N_ODD, C_WIDTH, D_MODEL), F32) * (C_WIDTH ** -0.5) * DN_BETA
    ln_g_odd = 1.0 + 0.02 * jax.random.normal(ks[24], (N_ODD, D_MODEL), F32)
    ln_b_odd = 0.02 * jax.random.normal(ks[25], (N_ODD, D_MODEL), F32)
    return {'x_prompt': x_prompt, 'x_sample': x_sample,
            'cache_swa_k': cache_swa_k, 'cache_swa_v': cache_swa_v,
            'state_conv': state_conv, 'state_ssm': state_ssm,
            'cache_fox_k': cache_fox_k, 'cache_fox_v': cache_fox_v, 'cache_fox_logf': cache_fox_logf,
            'page_table': page_table,
            'w_in_even': w_in_even, 'attn_sinks': attn_sinks, 'conv_w': conv_w, 'conv_b': conv_b,
            'dt_bias': dt_bias, 'a_log': a_log, 'd_skip': d_skip, 'ssm_norm_w': ssm_norm_w,
            'w_out_even': w_out_even, 'ln_g_even': ln_g_even, 'ln_b_even': ln_b_even,
            'w_in_odd': w_in_odd, 'forget_bias': forget_bias, 'w_out_odd': w_out_odd,
            'ln_g_odd': ln_g_odd, 'ln_b_odd': ln_b_odd}


def reference(x_prompt, x_sample, cache_swa_k, cache_swa_v, state_conv, state_ssm,
              cache_fox_k, cache_fox_v, cache_fox_logf, page_table,
              w_in_even, attn_sinks, conv_w, conv_b, dt_bias, a_log, d_skip, ssm_norm_w,
              w_out_even, ln_g_even, ln_b_even,
              w_in_odd, forget_bias, w_out_odd, ln_g_odd, ln_b_odd):
    pos_p = jnp.arange(x_prompt.shape[1], dtype=jnp.int32)
    pos_s = PAST_LEN + jnp.arange(x_sample.shape[1], dtype=jnp.int32)
    yp, ys = x_prompt, x_sample
    swa_kp, swa_vp, swa_ks, swa_vs, conv_p, conv_s, ssm_p, ssm_s = [], [], [], [], [], [], [], []
    fox_kp, fox_vp, fox_lp, fox_ks, fox_vs, fox_ls = [], [], [], [], [], []
    for layer in range(DEPTH):
        i = layer // 2
        if layer % 2 == 0:
            ew = (w_in_even[i], attn_sinks[i], conv_w[i], conv_b[i], dt_bias[i], a_log[i], d_skip[i],
                  ssm_norm_w[i], w_out_even[i], ln_g_even[i], ln_b_even[i])
            yp, kp, vp, cp, hp = _even_layer(yp, pos_p, False, None, None, None, None, *ew)
            ys, k_s, v_s, c_s, h_s = _even_layer(ys, pos_s, True, cache_swa_k[i], cache_swa_v[i],
                                                 state_conv[i], state_ssm[i], *ew)
            swa_kp.append(kp); swa_vp.append(vp); swa_ks.append(k_s); swa_vs.append(v_s)
            conv_p.append(cp); conv_s.append(c_s); ssm_p.append(hp); ssm_s.append(h_s)
        else:
            ow = (w_in_odd[i], forget_bias[i], w_out_odd[i], ln_g_odd[i], ln_b_odd[i])
            yp, kp, vp, lp = _odd_layer(yp, False, None, None, None, *ow)
            past_k = cache_fox_k[i, page_table]
            past_v = cache_fox_v[i, page_table]
            past_lf = cache_fox_logf[i, page_table]
            ys, k_s, v_s, l_s = _odd_layer(ys, True, past_k, past_v, past_lf, *ow)
            fox_kp.append(kp); fox_vp.append(vp); fox_lp.append(lp)
            fox_ks.append(k_s); fox_vs.append(v_s); fox_ls.append(l_s)
    return (yp, ys,
            jnp.stack(swa_kp), jnp.stack(swa_vp), jnp.stack(swa_ks), jnp.stack(swa_vs),
            jnp.stack(conv_p), jnp.stack(conv_s), jnp.stack(ssm_p), jnp.stack(ssm_s),
            jnp.stack(fox_kp), jnp.stack(fox_vp), jnp.stack(fox_lp),
            jnp.stack(fox_ks), jnp.stack(fox_vs), jnp.stack(fox_ls))
```

```python
import functools

import numpy as np
import jax
import jax.numpy as jnp
from jax import lax
from jax.experimental import pallas as pl
from jax.experimental.pallas import tpu as pltpu

F32 = jnp.float32
BF16 = jnp.bfloat16

D_MODEL = 1024
DEPTH = 2
HEAD_DIM = 64
ATTN_SCALE = HEAD_DIM ** -0.5
A_HEADS, A_KV_HEADS = 8, 2
A_REP = A_HEADS // A_KV_HEADS
A_WIDTH, A_KV_WIDTH = A_HEADS * HEAD_DIM, A_KV_HEADS * HEAD_DIM
WINDOW = 128
ROPE_THETA = 10000.0
B_HEADS, B_HEAD_DIM, B_GROUPS = 8, 64, 2
B_REP = B_HEADS // B_GROUPS
B_WIDTH = B_HEADS * B_HEAD_DIM
D_STATE = 128
CONV_WIDTH = 4
CONV_DIM = B_WIDTH + 2 * B_GROUPS * D_STATE
SSD_CHUNK = 128
C_HEADS, C_KV_HEADS = 16, 4
C_REP = C_HEADS // C_KV_HEADS
C_WIDTH, C_KV_WIDTH = C_HEADS * HEAD_DIM, C_KV_HEADS * HEAD_DIM
PAGE_SIZE = 128
DN_ALPHA = (2 * DEPTH) ** 0.25
LN_EPS = 1e-5
RMS_EPS = 1e-5
EVEN_SPLITS = (A_WIDTH, A_KV_WIDTH, A_KV_WIDTH, A_WIDTH, B_WIDTH, CONV_DIM, B_HEADS)
ODD_SPLITS = (C_WIDTH, C_KV_WIDTH, C_KV_WIDTH, C_HEADS, C_WIDTH)

LANES = 128
VMEM_LIMIT = 56 * 1024 * 1024
C_PIECES = 3
FOX_TQ = 128
FOX_TK = 512
DEC_PAGES = 16


def _cparams(*sem):
    return pltpu.CompilerParams(dimension_semantics=sem, vmem_limit_bytes=VMEM_LIMIT)


def _const_spec(shape):
    nd = len(shape)
    return pl.BlockSpec(shape, lambda *_: (0,) * nd)


def _silu(x):
    return x * (1.0 / (1.0 + jnp.exp(-x)))


def _softplus(x):
    return jnp.maximum(x, 0.0) + jnp.log1p(jnp.exp(-jnp.abs(x)))


def _log_sigmoid(x):
    return jnp.minimum(x, 0.0) - jnp.log1p(jnp.exp(-jnp.abs(x)))


def _lane_iota(shape):
    return lax.broadcasted_iota(jnp.int32, shape, len(shape) - 1)


def _dot(a, b):
    return jnp.dot(a, b, preferred_element_type=F32)


def _dot_nt(a, b):
    return lax.dot_general(a, b, (((1,), (1,)), ((), ())), preferred_element_type=F32)


def _dot_tn(a, b):
    return lax.dot_general(a, b, (((0,), (0,)), ((), ())), preferred_element_type=F32)


def _split_heads_to_lane_tiles(x, n_pairs):
    out = []
    for j in range(n_pairs):
        col = x[:, j * LANES:(j + 1) * LANES]
        low = _lane_iota(col.shape) < HEAD_DIM
        out.append(jnp.where(low, col, 0.0))
        out.append(jnp.where(low, pltpu.roll(col, HEAD_DIM, axis=1), 0.0))
    return out


def _merge_heads_from_lane_tiles(tiles):
    cols = []
    for j in range(len(tiles) // 2):
        a, b = tiles[2 * j], tiles[2 * j + 1]
        low = _lane_iota(a.shape) < HEAD_DIM
        cols.append(jnp.where(low, a, pltpu.roll(b, HEAD_DIM, axis=1)))
    return jnp.concatenate(cols, axis=1) if len(cols) > 1 else cols[0]


def _rope_cols(x, cos, sin_signed):
    half = HEAD_DIM // 2
    first = (_lane_iota(x.shape) % HEAD_DIM) < half
    swapped = jnp.where(first, pltpu.roll(x, LANES - half, axis=1), pltpu.roll(x, half, axis=1))
    return x * cos + swapped * sin_signed


def _even_inproj_kernel(x_ref, cos_ref, sin_ref, wq, wk, wv, wg, wz, wx, wdt,
                        q_o, k_o, v_o, g_o, z_o, xbc_o, dt_o):
    xb = x_ref[...].astype(BF16)
    cos, sin = cos_ref[...], sin_ref[...]
    q = _dot(xb, wq[...])
    for j in range(A_WIDTH // LANES):
        sl = slice(j * LANES, (j + 1) * LANES)
        q_o[:, sl] = (_rope_cols(q[:, sl], cos, sin) * ATTN_SCALE).astype(q_o.dtype)
    k_o[...] = _rope_cols(_dot(xb, wk[...]), cos, sin)
    v_o[...] = _dot(xb, wv[...])
    g_o[...] = _dot(xb, wg[...]).astype(g_o.dtype)
    z_o[...] = _dot(xb, wz[...]).astype(z_o.dtype)
    xbc_o[...] = _dot(xb, wx[...])
    dt_o[...] = _dot(xb, wdt[...])


def _even_inproj(x2d, cos_tab, sin_tab, w, tm, act_dtype):
    n = x2d.shape[0]
    nblk_seq = cos_tab.shape[0] // tm
    row = lambda i: (i, 0)
    outs = [((n, A_WIDTH), act_dtype), ((n, A_KV_WIDTH), F32), ((n, A_KV_WIDTH), F32),
            ((n, A_WIDTH), act_dtype), ((n, B_WIDTH), act_dtype), ((n, CONV_DIM), F32), ((n, LANES), F32)]
    return pl.pallas_call(
        _even_inproj_kernel,
        grid=(n // tm,),
        in_specs=[pl.BlockSpec((tm, D_MODEL), row),
                  pl.BlockSpec((tm, LANES), lambda i: (i % nblk_seq, 0)),
                  pl.BlockSpec((tm, LANES), lambda i: (i % nblk_seq, 0))]
                 + [_const_spec(a.shape) for a in w],
        out_specs=[pl.BlockSpec((tm, s[1]), row) for s, _ in outs],
        out_shape=[jax.ShapeDtypeStruct(s, d) for s, d in outs],
        compiler_params=_cparams("parallel"),
        name="even_inproj",
    )(x2d, cos_tab, sin_tab, *w)


def _swa_prompt_kernel(sinks_ref, q_ref, kp_ref, kc_ref, vp_ref, vc_ref, g_ref, o_ref):
    n = pl.program_id(1)
    q = q_ref[...]
    kk = jnp.concatenate([kp_ref[...], kc_ref[...]], axis=0).astype(BF16)
    vv = jnp.concatenate([vp_ref[...], vc_ref[...]], axis=0).astype(BF16)
    qpos = WINDOW + lax.broadcasted_iota(jnp.int32, (WINDOW, 2 * WINDOW), 0)
    kpos = lax.broadcasted_iota(jnp.int32, (WINDOW, 2 * WINDOW), 1)
    diff = qpos - kpos
    mask = (diff >= 0) & (diff <= WINDOW) & ((kpos >= WINDOW) | (n > 0))
    heads = []
    for h in range(A_HEADS):
        kv = h // A_REP
        qh = q[:, h * HEAD_DIM:(h + 1) * HEAD_DIM]
        kh = kk[:, kv * HEAD_DIM:(kv + 1) * HEAD_DIM]
        vh = vv[:, kv * HEAD_DIM:(kv + 1) * HEAD_DIM]
        s = jnp.where(mask, _dot_nt(qh, kh), -jnp.inf)
        sink = sinks_ref[h]
        m = jnp.maximum(jnp.max(s, axis=-1, keepdims=True), sink)
        p = jnp.exp(s - m)
        den = jnp.sum(p, axis=-1, keepdims=True) + jnp.exp(sink - m)
        heads.append(_dot((p / den).astype(BF16), vh))
    o = jnp.concatenate(heads, axis=1)
    o_ref[...] = (o * _silu(g_ref[...].astype(F32))).astype(o_ref.dtype)


def _swa_prompt(q, k, v, g, sinks, batch, seq):
    nb = seq // WINDOW
    cur = lambda b, n: (b * nb + n, 0)
    prev = lambda b, n: (b * nb + jnp.maximum(n - 1, 0), 0)
    return pl.pallas_call(
        _swa_prompt_kernel,
        grid=(batch, nb),
        in_specs=[pl.BlockSpec(memory_space=pltpu.SMEM),
                  pl.BlockSpec((WINDOW, A_WIDTH), cur),
                  pl.BlockSpec((WINDOW, A_KV_WIDTH), prev),
                  pl.BlockSpec((WINDOW, A_KV_WIDTH), cur),
                  pl.BlockSpec((WINDOW, A_KV_WIDTH), prev),
                  pl.BlockSpec((WINDOW, A_KV_WIDTH), cur),
                  pl.BlockSpec((WINDOW, A_WIDTH), cur)],
        out_specs=pl.BlockSpec((WINDOW, A_WIDTH), cur),
        out_shape=jax.ShapeDtypeStruct((batch * seq, A_WIDTH), BF16),
        compiler_params=_cparams("parallel", "parallel"),
        name="swa_prompt",
    )(sinks, q, k, k, v, v, g)


def _swa_sample_kernel(sinks_ref, q_ref, kn_ref, vn_ref, ck_ref, cv_ref, g_ref, o_ref, nk_ref, nv_ref):
    ck, cv = ck_ref[...], cv_ref[...]
    kn, vn = kn_ref[...], vn_ref[...]
    nk_ref[:, 0:WINDOW - 1, :] = ck[:, 1:WINDOW, :]
    nk_ref[:, WINDOW - 1:WINDOW, :] = kn
    nv_ref[:, 0:WINDOW - 1, :] = cv[:, 1:WINDOW, :]
    nv_ref[:, WINDOW - 1:WINDOW, :] = vn
    q = q_ref[...]
    g = g_ref[...]
    hrow = lax.broadcasted_iota(jnp.int32, (A_REP, 1), 0)
    for kv in range(A_KV_HEADS):
        hs = slice(kv * A_REP, (kv + 1) * A_REP)
        ds = slice(kv * HEAD_DIM, (kv + 1) * HEAD_DIM)
        qh = q[:, hs, :].astype(BF16)
        kh, vh = ck[:, :, ds].astype(BF16), cv[:, :, ds].astype(BF16)
        knh, vnh = kn[:, :, ds].astype(BF16).astype(F32), vn[:, :, ds].astype(BF16).astype(F32)
        s_c = jnp.einsum("bqd,bkd->bqk", qh, kh, preferred_element_type=F32)
        s_n = jnp.sum(qh.astype(F32) * knh, axis=-1, keepdims=True)
        sink = jnp.zeros((A_REP, 1), F32)
        for r in range(A_REP):
            sink = jnp.where(hrow == r, sinks_ref[kv * A_REP + r], sink)
        sink = sink[None]
        m = jnp.maximum(jnp.maximum(jnp.max(s_c, axis=-1, keepdims=True), s_n), sink)
        p_c = jnp.exp(s_c - m)
        p_n = jnp.exp(s_n - m)
        den = jnp.sum(p_c, axis=-1, keepdims=True) + p_n + jnp.exp(sink - m)
        p_c = (p_c / den).astype(BF16)
        p_n = (p_n / den).astype(BF16).astype(F32)
        o = jnp.einsum("bqk,bkd->bqd", p_c, vh, preferred_element_type=F32) + p_n * vnh
        o_ref[:, hs, :] = (o * _silu(g[:, hs, :])).astype(o_ref.dtype)


def _swa_sample(q3, k_new, v_new, cache_k, cache_v, g3, sinks, bb):
    nb = q3.shape[0]
    blk3 = lambda i: (i, 0, 0)
    return pl.pallas_call(
        _swa_sample_kernel,
        grid=(nb // bb,),
        in_specs=[pl.BlockSpec(memory_space=pltpu.SMEM),
                  pl.BlockSpec((bb, A_HEADS, HEAD_DIM), blk3),
                  pl.BlockSpec((bb, 1, A_KV_WIDTH), blk3),
                  pl.BlockSpec((bb, 1, A_KV_WIDTH), blk3),
                  pl.BlockSpec((bb, WINDOW, A_KV_WIDTH), blk3),
                  pl.BlockSpec((bb, WINDOW, A_KV_WIDTH), blk3),
                  pl.BlockSpec((bb, A_HEADS, HEAD_DIM), blk3)],
        out_specs=[pl.BlockSpec((bb, A_HEADS, HEAD_DIM), blk3),
                   pl.BlockSpec((bb, WINDOW, A_KV_WIDTH), blk3),
                   pl.BlockSpec((bb, WINDOW, A_KV_WIDTH), blk3)],
        out_shape=[jax.ShapeDtypeStruct((nb, A_HEADS, HEAD_DIM), F32),
                   jax.ShapeDtypeStruct((nb, WINDOW, A_KV_WIDTH), F32),
                   jax.ShapeDtypeStruct((nb, WINDOW, A_KV_WIDTH), F32)],
        compiler_params=_cparams("parallel"),
        name="swa_sample",
    )(sinks, q3, k_new, v_new, cache_k, cache_v, g3)


def _gated_rmsnorm(y, z, w):
    g = y * _silu(z)
    gw = B_WIDTH // B_GROUPS
    outs = []
    for i in range(B_GROUPS):
        gi = g[..., i * gw:(i + 1) * gw]
        outs.append(gi * lax.rsqrt(jnp.mean(gi * gi, axis=-1, keepdims=True) + RMS_EPS))
    return jnp.concatenate(outs, axis=-1) * w


def _ssd_prompt_kernel(dskip_ref, xbc_ref, dt_ref, z_ref, cw_ref, cb_ref, dtb_ref, alog_ref, nw_ref,
                       o_ref, state_ref, xp_ref):
    L = SSD_CHUNK
    c = pl.program_id(1)

    @pl.when(c == 0)
    def _():
        state_ref[...] = jnp.zeros_like(state_ref)
        xp_ref[0:8, :] = jnp.zeros((8, CONV_DIM), F32)

    x = xbc_ref[...]
    xp_ref[8:8 + L, :] = x
    acc = cb_ref[...] + xp_ref[pl.ds(8 - (CONV_WIDTH - 1), L), :] * cw_ref[0:1, :]
    for j in range(1, CONV_WIDTH):
        acc = acc + xp_ref[pl.ds(8 - (CONV_WIDTH - 1) + j, L), :] * cw_ref[j:j + 1, :]
    xp_ref[0:8, :] = x[L - 8:L, :]
    u = _silu(acc)
    xs = u[:, :B_WIDTH]
    bm = u[:, B_WIDTH:B_WIDTH + B_GROUPS * D_STATE].astype(BF16)
    cm = u[:, B_WIDTH + B_GROUPS * D_STATE:].astype(BF16)

    dt = _softplus(dt_ref[...] + dtb_ref[...])
    da = dt * (-jnp.exp(alog_ref[...]))
    row = lax.broadcasted_iota(jnp.int32, (L, L), 0)
    col = lax.broadcasted_iota(jnp.int32, (L, L), 1)
    causal = row >= col
    cum = jnp.dot(causal.astype(F32), da, precision=lax.Precision.HIGHEST, preferred_element_type=F32)
    cum_t = cum.T
    cb = [_dot_nt(cm[:, g * D_STATE:(g + 1) * D_STATE], bm[:, g * D_STATE:(g + 1) * D_STATE])
          for g in range(B_GROUPS)]

    ys = []
    for h in range(B_HEADS):
        g = h // B_REP
        cum_col = cum[:, h:h + 1]
        cum_row = cum_t[h:h + 1, :]
        cum_end = cum[L - 1:L, h:h + 1]
        xs_h = xs[:, h * B_HEAD_DIM:(h + 1) * B_HEAD_DIM]
        xdt = xs_h * dt[:, h:h + 1]
        decay = jnp.exp(jnp.where(causal, cum_col - cum_row, -jnp.inf))
        y = _dot((cb[g] * decay).astype(BF16), xdt.astype(BF16))
        st = state_ref[0, h]
        cm_g = cm[:, g * D_STATE:(g + 1) * D_STATE]
        y = y + _dot_nt(cm_g, st.astype(BF16)) * jnp.exp(cum_col)
        ys.append(y + dskip_ref[h] * xs_h)
        xw = (xdt * jnp.exp(cum_end - cum_col)).astype(BF16)
        state_ref[0, h] = st * jnp.exp(cum_end) + _dot_tn(xw, bm[:, g * D_STATE:(g + 1) * D_STATE])
    y = jnp.concatenate(ys, axis=1)
    o_ref[...] = _gated_rmsnorm(y, z_ref[...].astype(F32), nw_ref[...]).astype(o_ref.dtype)


def _ssd_prompt(xbc, dt, z, conv_w, conv_b, dt_bias, a_log, d_skip, norm_w, batch, seq):
    nc = seq // SSD_CHUNK
    cur = lambda b, c: (b * nc + c, 0)
    return pl.pallas_call(
        _ssd_prompt_kernel,
        grid=(batch, nc),
        in_specs=[pl.BlockSpec(memory_space=pltpu.SMEM),
                  pl.BlockSpec((SSD_CHUNK, CONV_DIM), cur),
                  pl.BlockSpec((SSD_CHUNK, LANES), cur),
                  pl.BlockSpec((SSD_CHUNK, B_WIDTH), cur),
                  _const_spec(conv_w.shape), _const_spec(conv_b.shape), _const_spec(dt_bias.shape),
                  _const_spec(a_log.shape), _const_spec(norm_w.shape)],
        out_specs=[pl.BlockSpec((SSD_CHUNK, B_WIDTH), cur),
                   pl.BlockSpec((1, B_HEADS, B_HEAD_DIM, D_STATE), lambda b, c: (b, 0, 0, 0))],
        out_shape=[jax.ShapeDtypeStruct((batch * seq, B_WIDTH), BF16),
                   jax.ShapeDtypeStruct((batch, B_HEADS, B_HEAD_DIM, D_STATE), F32)],
        scratch_shapes=[pltpu.VMEM((8 + SSD_CHUNK, CONV_DIM), F32)],
        compiler_params=_cparams("parallel", "arbitrary"),
        name="ssd_prompt",
    )(d_skip, xbc, dt, z, conv_w, conv_b, dt_bias, a_log, norm_w)


def _ssd_sample_kernel(dskip_ref, xbc_ref, hist_ref, dt_ref, z_ref, h0_ref, cw_ref, cb_ref, dtb_ref,
                       alog_ref, nw_ref, o_ref, conv_o, state_o):
    x = xbc_ref[...]
    hist = hist_ref[...]
    acc = cb_ref[...][None] + x * cw_ref[CONV_WIDTH - 1:CONV_WIDTH, :][None]
    for j in range(CONV_WIDTH - 1):
        acc = acc + hist[:, j:j + 1, :] * cw_ref[j:j + 1, :][None]
    conv_o[:, 0:CONV_WIDTH - 2, :] = hist[:, 1:CONV_WIDTH - 1, :]
    conv_o[:, CONV_WIDTH - 2:CONV_WIDTH - 1, :] = x
    u = _silu(acc)
    xs = u[:, :, :B_WIDTH]
    bm = u[:, :, B_WIDTH:B_WIDTH + B_GROUPS * D_STATE]
    cm = u[:, :, B_WIDTH + B_GROUPS * D_STATE:]
    dt = _softplus(dt_ref[...] + dtb_ref[...][None])
    dec = jnp.exp(dt * (-jnp.exp(alog_ref[...]))[None])
    ys = []
    for h in range(B_HEADS):
        g = h // B_REP
        xs_h = xs[:, :, h * B_HEAD_DIM:(h + 1) * B_HEAD_DIM]
        xdt = xs_h * dt[:, :, h:h + 1]
        bm_g = bm[:, :, g * D_STATE:(g + 1) * D_STATE]
        cm_g = cm[:, :, g * D_STATE:(g + 1) * D_STATE]
        dec_h = dec[:, :, h:h + 1]
        st = h0_ref[:, h]
        cb = jnp.sum(cm_g * bm_g, axis=-1, keepdims=True)
        y_off = jnp.einsum("bqn,bpn->bqp", cm_g.astype(BF16), st.astype(BF16),
                           preferred_element_type=F32)
        ys.append(cb * xdt + y_off * dec_h + dskip_ref[h] * xs_h)
        outer = jnp.einsum("bqp,bqn->bpn", xdt.astype(BF16), bm_g.astype(BF16),
                           preferred_element_type=F32)
        state_o[:, h] = st * dec_h + outer
    y = jnp.concatenate(ys, axis=-1)
    o_ref[...] = _gated_rmsnorm(y, z_ref[...], nw_ref[...][None]).astype(o_ref.dtype)


def _ssd_sample(xbc3, hist, dt3, z3, h0, conv_w, conv_b, dt_bias, a_log, d_skip, norm_w, bb):
    nb = xbc3.shape[0]
    b3 = lambda i: (i, 0, 0)
    b4 = lambda i: (i, 0, 0, 0)
    return pl.pallas_call(
        _ssd_sample_kernel,
        grid=(nb // bb,),
        in_specs=[pl.BlockSpec(memory_space=pltpu.SMEM),
                  pl.BlockSpec((bb, 1, CONV_DIM), b3),
                  pl.BlockSpec((bb, CONV_WIDTH - 1, CONV_DIM), b3),
                  pl.BlockSpec((bb, 1, LANES), b3),
                  pl.BlockSpec((bb, 1, B_WIDTH), b3),
                  pl.BlockSpec((bb, B_HEADS, B_HEAD_DIM, D_STATE), b4),
                  _const_spec(conv_w.shape), _const_spec(conv_b.shape), _const_spec(dt_bias.shape),
                  _const_spec(a_log.shape), _const_spec(norm_w.shape)],
        out_specs=[pl.BlockSpec((bb, 1, B_WIDTH), b3),
                   pl.BlockSpec((bb, CONV_WIDTH - 1, CONV_DIM), b3),
                   pl.BlockSpec((bb, B_HEADS, B_HEAD_DIM, D_STATE), b4)],
        out_shape=[jax.ShapeDtypeStruct((nb, 1, B_WIDTH), F32),
                   jax.ShapeDtypeStruct((nb, CONV_WIDTH - 1, CONV_DIM), F32),
                   jax.ShapeDtypeStruct((nb, B_HEADS, B_HEAD_DIM, D_STATE), F32)],
        compiler_params=_cparams("parallel"),
        name="ssd_sample",
    )(d_skip, xbc3, hist, dt3, z3, h0, conv_w, conv_b, dt_bias, a_log, norm_w)


def _outproj_ln_kernel(n_parts, x_ref, *refs):
    parts, ws = refs[:n_parts], refs[n_parts:2 * n_parts]
    g_ref, b_ref, o_ref = refs[2 * n_parts:]
    mix = _dot(parts[0][...].astype(BF16), ws[0][...])
    for a, w in zip(parts[1:], ws[1:]):
        mix = mix + _dot(a[...].astype(BF16), w[...])
    y = DN_ALPHA * x_ref[...] + mix
    mu = jnp.mean(y, axis=-1, keepdims=True)
    var = jnp.mean(jnp.square(y - mu), axis=-1, keepdims=True)
    o_ref[...] = (y - mu) * lax.rsqrt(var + LN_EPS) * g_ref[...] + b_ref[...]


def _outproj_ln(x2d, parts, ws, ln_g, ln_b, tm):
    n = x2d.shape[0]
    row = lambda i: (i, 0)
    return pl.pallas_call(
        functools.partial(_outproj_ln_kernel, len(parts)),
        grid=(n // tm,),
        in_specs=[pl.BlockSpec((tm, D_MODEL), row)]
                 + [pl.BlockSpec((tm, p.shape[1]), row) for p in parts]
                 + [_const_spec(w.shape) for w in ws]
                 + [_const_spec(ln_g.shape), _const_spec(ln_b.shape)],
        out_specs=pl.BlockSpec((tm, D_MODEL), row),
        out_shape=jax.ShapeDtypeStruct((n, D_MODEL), F32),
        compiler_params=_cparams("parallel"),
        name="outproj_ln",
    )(x2d, *parts, *ws, ln_g, ln_b)


def _bf16_pieces(c):
    pieces, rem = [], c
    for _ in range(C_PIECES):
        p = rem.astype(BF16).astype(F32)
        pieces.append(p)
        rem = rem - p
    return pieces


def _odd_inproj_prompt_kernel(tiles_per_seq, x_ref, wq, wk, wv, wf, wg, fb_ref, sq_ref, sk_ref, cq_ref, ck_ref,
                              qa_o, ka_o, vt_o, k_o, v_o, lf_o, g_o, carry_ref):
    i = pl.program_id(0)
    tm = x_ref.shape[0]

    @pl.when(i % tiles_per_seq == 0)
    def _():
        carry_ref[...] = jnp.zeros_like(carry_ref)

    xb = x_ref[...].astype(BF16)
    kf = _dot(xb, wk[...])
    vf = _dot(xb, wv[...])
    k_o[...] = kf
    v_o[...] = vf
    g_o[...] = _dot(xb, wg[...]).astype(g_o.dtype)
    lf = _log_sigmoid(_dot(xb, wf[...]) + fb_ref[...])
    lf_o[...] = lf[:, :C_HEADS]

    c = jnp.where(_lane_iota(lf.shape) < C_HEADS, lf, 0.0)
    rows = lax.broadcasted_iota(jnp.int32, c.shape, 0)
    s = 1
    while s < tm:
        c = c + jnp.where(rows >= s, pltpu.roll(c, s, axis=0), 0.0)
        s *= 2
    c = c + carry_ref[...]
    carry_ref[...] = c[tm - 1:tm, :]

    hi, mid, lo = _bf16_pieces(c)
    c3 = (hi + pltpu.roll(mid, C_HEADS, axis=1) + pltpu.roll(lo, 2 * C_HEADS, axis=1)).astype(BF16)
    ex_q = _dot(c3, sq_ref[...]) + cq_ref[...]
    ex_k = _dot(c3, sk_ref[...]) + ck_ref[...]

    qf = _dot(xb, wq[...])
    for h, t in enumerate(_split_heads_to_lane_tiles(qf, C_WIDTH // LANES)):
        sl = slice(h * LANES, (h + 1) * LANES)
        qa_o[:, sl] = (t + ex_q[:, sl]).astype(qa_o.dtype)
    ones_row = (_lane_iota((1, LANES)) == HEAD_DIM).astype(F32)
    k_tiles = _split_heads_to_lane_tiles(kf, C_KV_WIDTH // LANES)
    v_tiles = _split_heads_to_lane_tiles(vf, C_KV_WIDTH // LANES)
    for g in range(C_KV_HEADS):
        sl = slice(g * LANES, (g + 1) * LANES)
        ka_o[:, sl] = (k_tiles[g] + ex_k[:, sl]).astype(ka_o.dtype)
        vt_o[0, g] = (v_tiles[g] + ones_row).T.astype(vt_o.dtype)


def _aug_selectors():
    sq = np.zeros((LANES, C_HEADS * LANES), np.float32)
    sk = np.zeros((LANES, C_KV_HEADS * LANES), np.float32)
    cq = np.zeros((1, C_HEADS * LANES), np.float32)
    ck = np.zeros((1, C_KV_HEADS * LANES), np.float32)
    for h in range(C_HEADS):
        g, r = divmod(h, C_REP)
        for p in range(C_PIECES):
            sq[p * C_HEADS + h, h * LANES + HEAD_DIM + p] = 1.0
            col = HEAD_DIM + C_PIECES + C_PIECES * r + p
            cq[0, h * LANES + col] = -1.0
            sk[p * C_HEADS + h, g * LANES + col] = 1.0
    for g in range(C_KV_HEADS):
        for p in range(C_PIECES):
            ck[0, g * LANES + HEAD_DIM + p] = 1.0
    return jnp.asarray(sq, BF16), jnp.asarray(sk, BF16), jnp.asarray(cq), jnp.asarray(ck)


def _odd_inproj_prompt(x2d, w, f_bias, batch, seq, tm):
    n = x2d.shape[0]
    tiles = seq // tm
    sq, sk, cq, ck = _aug_selectors()
    row = lambda i: (i, 0)
    consts = list(w) + [f_bias, sq, sk, cq, ck]
    outs = [((n, C_HEADS * LANES), BF16), ((n, C_KV_HEADS * LANES), BF16),
            ((batch, C_KV_HEADS, LANES, seq), BF16), ((n, C_KV_WIDTH), F32), ((n, C_KV_WIDTH), F32),
            ((n, C_HEADS), F32), ((n, C_WIDTH), BF16)]
    out_specs = [pl.BlockSpec((tm, outs[0][0][1]), row), pl.BlockSpec((tm, outs[1][0][1]), row),
                 pl.BlockSpec((1, C_KV_HEADS, LANES, tm), lambda i: (i // tiles, 0, 0, i % tiles)),
                 pl.BlockSpec((tm, C_KV_WIDTH), row), pl.BlockSpec((tm, C_KV_WIDTH), row),
                 pl.BlockSpec((tm, C_HEADS), row), pl.BlockSpec((tm, C_WIDTH), row)]
    return pl.pallas_call(
        functools.partial(_odd_inproj_prompt_kernel, tiles),
        grid=(n // tm,),
        in_specs=[pl.BlockSpec((tm, D_MODEL), row)] + [_const_spec(a.shape) for a in consts],
        out_specs=out_specs,
        out_shape=[jax.ShapeDtypeStruct(s, d) for s, d in outs],
        scratch_shapes=[pltpu.VMEM((1, LANES), F32)],
        compiler_params=_cparams("arbitrary"),
        name="odd_inproj_prompt",
    )(x2d, *consts)


def _odd_inproj_sample_kernel(x_ref, wq, wk, wv, wf, wg, fb_ref, q_o, k_o, v_o, lf_o, g_o):
    xb = x_ref[...].astype(BF16)
    q_o[...] = _dot(xb, wq[...])
    k_o[...] = _dot(xb, wk[...])
    v_o[...] = _dot(xb, wv[...])
    lf_o[...] = _log_sigmoid(_dot(xb, wf[...]) + fb_ref[...])[:, :C_HEADS]
    g_o[...] = _dot(xb, wg[...])


def _odd_inproj_sample(x2d, w, f_bias):
    n = x2d.shape[0]
    outs = [((n, C_WIDTH), F32), ((n, C_KV_WIDTH), F32), ((n, C_KV_WIDTH), F32), ((n, C_HEADS), F32),
            ((n, C_WIDTH), F32)]
    return pl.pallas_call(
        _odd_inproj_sample_kernel,
        out_shape=[jax.ShapeDtypeStruct(s, d) for s, d in outs],
        compiler_params=pltpu.CompilerParams(vmem_limit_bytes=VMEM_LIMIT),
        name="odd_inproj_sample",
    )(x2d, *w, f_bias)


def _fox_prompt_kernel(q_ref, k_ref, vt_ref, g_ref, o_ref, m_ref, acc_ref):
    qi = pl.program_id(2)
    tq, tk = FOX_TQ, FOX_TK
    qa = q_ref[...]
    q_stack = jnp.concatenate([qa[:, r * LANES:(r + 1) * LANES] for r in range(C_REP)], axis=0)
    m_ref[...] = jnp.full(m_ref.shape, -jnp.inf, F32)
    acc_ref[...] = jnp.zeros_like(acc_ref)
    n_full = (qi * tq) // tk

    def step(kj, masked):
        start = pl.multiple_of(kj * tk, tk)
        s_t = _dot_nt(k_ref[pl.ds(start, tk), :], q_stack)
        if masked:
            kpos = start + lax.broadcasted_iota(jnp.int32, s_t.shape, 0)
            qpos = qi * tq + lax.broadcasted_iota(jnp.int32, s_t.shape, 1) % tq
            s_t = jnp.where(kpos <= qpos, s_t, -jnp.inf)
        m_old = m_ref[...]
        m_new = jnp.maximum(m_old, jnp.max(s_t, axis=0, keepdims=True))
        p = jnp.exp(s_t - m_new).astype(BF16)
        acc_ref[...] = acc_ref[...] * jnp.exp(m_old - m_new) + _dot(vt_ref[0, 0, :, pl.ds(start, tk)], p)
        m_ref[...] = m_new

    def body(kj, carry):
        step(kj, False)
        return carry

    lax.fori_loop(0, n_full, body, 0)
    step(n_full, True)

    acc = acc_ref[...]
    tiles = []
    for r in range(C_REP):
        t = acc[:, r * tq:(r + 1) * tq].T
        tiles.append(t / t[:, HEAD_DIM:HEAD_DIM + 1])
    o = _merge_heads_from_lane_tiles(tiles)
    o_ref[...] = (o * _silu(g_ref[...].astype(F32))).astype(o_ref.dtype)


def _fox_prompt(q_aug, k_aug, vt_aug, g, batch, seq):
    nq = seq // FOX_TQ
    gw = C_REP * HEAD_DIM
    return pl.pallas_call(
        _fox_prompt_kernel,
        grid=(batch, C_KV_HEADS, nq),
        in_specs=[pl.BlockSpec((FOX_TQ, C_REP * LANES), lambda b, g, i: (b * nq + i, g)),
                  pl.BlockSpec((seq, LANES), lambda b, g, i: (b, g)),
                  pl.BlockSpec((1, 1, LANES, seq), lambda b, g, i: (b, g, 0, 0)),
                  pl.BlockSpec((FOX_TQ, gw), lambda b, g, i: (b * nq + i, g))],
        out_specs=pl.BlockSpec((FOX_TQ, gw), lambda b, g, i: (b * nq + i, g)),
        out_shape=jax.ShapeDtypeStruct((batch * seq, C_WIDTH), BF16),
        scratch_shapes=[pltpu.VMEM((1, C_REP * FOX_TQ), F32), pltpu.VMEM((LANES, C_REP * FOX_TQ), F32)],
        compiler_params=_cparams("parallel", "parallel", "arbitrary"),
        name="fox_prompt",
    )(q_aug, k_aug, vt_aug, g)


def _fox_decode_kernel(pt_ref, q_ref, kn_ref, vn_ref, lfn_ref, g_ref, k_hbm, v_hbm, lf_hbm, o_ref,
                       kbuf, vbuf, lfbuf, sem, qm_ref, m_ref, l_ref, acc_ref, carry_ref):
    b, c = pl.program_id(0), pl.program_id(1)
    nb, nc = pl.num_programs(0), pl.num_programs(1)
    G = DEC_PAGES
    n_pages = nc * G
    step = b * nc + c
    slot = step % 2

    def copies(bb, cc, sl):
        out = []
        for g in range(G):
            page = pt_ref[bb, n_pages - (cc + 1) * G + g]
            lanes = pl.ds(g * PAGE_SIZE, PAGE_SIZE)
            out.append(pltpu.make_async_copy(k_hbm.at[page], kbuf.at[sl, :, lanes], sem.at[0, sl]))
            out.append(pltpu.make_async_copy(v_hbm.at[page], vbuf.at[sl, :, lanes], sem.at[1, sl]))
            out.append(pltpu.make_async_copy(lf_hbm.at[page], lfbuf.at[sl, :, lanes], sem.at[2, sl]))
        return out

    @pl.when(step == 0)
    def _():
        for cp in copies(0, 0, 0):
            cp.start()

    @pl.when(step + 1 < nb * nc)
    def _():
        nxt = step + 1
        for cp in copies(nxt // nc, nxt % nc, 1 - slot):
            cp.start()

    @pl.when(c == 0)
    def _():
        q16 = q_ref[0]
        q4 = jnp.concatenate([q16] * C_KV_HEADS, axis=1)
        hrow = lax.broadcasted_iota(jnp.int32, q4.shape, 0) // C_REP
        gcol = _lane_iota(q4.shape) // HEAD_DIM
        qm = jnp.where(hrow == gcol, q4, 0.0).astype(BF16)
        qm_ref[...] = qm
        kn = kn_ref[0].astype(BF16).astype(F32)
        m_ref[...] = jnp.sum(qm.astype(F32) * kn, axis=-1, keepdims=True)
        l_ref[...] = jnp.ones_like(l_ref)
        acc_ref[...] = jnp.broadcast_to(vn_ref[0].astype(BF16).astype(F32), acc_ref.shape)
        carry_ref[...] = lfn_ref[0]

    for cp in copies(b, c, slot):
        cp.wait()

    lf = lfbuf[slot]
    lf_rows = jnp.concatenate([lf[:, g * PAGE_SIZE:(g + 1) * PAGE_SIZE] for g in range(G)], axis=0)
    hi = lf_rows.astype(BF16)
    lo = (lf_rows - hi.astype(F32)).astype(BF16)
    ii = lax.broadcasted_iota(jnp.int32, (PAGE_SIZE, PAGE_SIZE), 0)
    jj = lax.broadcasted_iota(jnp.int32, (PAGE_SIZE, PAGE_SIZE), 1)
    later = (ii > jj).astype(BF16)
    within = _dot(hi, later) + _dot(lo, later)
    carry = carry_ref[...]
    cols = [None] * G
    for g in reversed(range(G)):
        rs = slice(g * C_HEADS, (g + 1) * C_HEADS)
        cols[g] = within[rs] + carry
        carry = carry + within[rs][:, 0:1] + lf_rows[rs][:, 0:1]
    carry_ref[...] = carry

    s = _dot(qm_ref[...], kbuf[slot].astype(BF16)) + jnp.concatenate(cols, axis=1)
    m_old = m_ref[...]
    m_new = jnp.maximum(m_old, jnp.max(s, axis=-1, keepdims=True))
    alpha = jnp.exp(m_old - m_new)
    p = jnp.exp(s - m_new)
    l_ref[...] = l_ref[...] * alpha + jnp.sum(p, axis=-1, keepdims=True)
    acc_ref[...] = acc_ref[...] * alpha + _dot_nt(p.astype(BF16), vbuf[slot].astype(BF16))
    m_ref[...] = m_new

    @pl.when(c == nc - 1)
    def _():
        o_all = acc_ref[...] / l_ref[...]
        hrow = lax.broadcasted_iota(jnp.int32, (C_HEADS, HEAD_DIM), 0) // C_REP
        o = jnp.zeros((C_HEADS, HEAD_DIM), F32)
        for g in range(C_KV_HEADS):
            o = jnp.where(hrow == g, o_all[:, g * HEAD_DIM:(g + 1) * HEAD_DIM], o)
        o_ref[0] = (o * _silu(g_ref[0])).astype(o_ref.dtype)


def _fox_decode(page_table, q3, k_new, v_new, lf_new, g3, cache_k, cache_v, cache_lf):
    nb, n_pages = page_table.shape
    nc = n_pages // DEC_PAGES
    b3 = lambda b, c, pt: (b, 0, 0)
    grid_spec = pltpu.PrefetchScalarGridSpec(
        num_scalar_prefetch=1,
        grid=(nb, nc),
        in_specs=[pl.BlockSpec((1, C_HEADS, HEAD_DIM), b3),
                  pl.BlockSpec((1, 1, C_KV_WIDTH), b3),
                  pl.BlockSpec((1, 1, C_KV_WIDTH), b3),
                  pl.BlockSpec((1, C_HEADS, 1), b3),
                  pl.BlockSpec((1, C_HEADS, HEAD_DIM), b3),
                  pl.BlockSpec(memory_space=pl.ANY),
                  pl.BlockSpec(memory_space=pl.ANY),
                  pl.BlockSpec(memory_space=pl.ANY)],
        out_specs=pl.BlockSpec((1, C_HEADS, HEAD_DIM), b3),
        scratch_shapes=[pltpu.VMEM((2, C_KV_WIDTH, DEC_PAGES * PAGE_SIZE), F32),
                        pltpu.VMEM((2, C_KV_WIDTH, DEC_PAGES * PAGE_SIZE), F32),
                        pltpu.VMEM((2, C_HEADS, DEC_PAGES * PAGE_SIZE), F32),
                        pltpu.SemaphoreType.DMA((3, 2)),
                        pltpu.VMEM((C_HEADS, C_KV_WIDTH), BF16),
                        pltpu.VMEM((C_HEADS, 1), F32),
                        pltpu.VMEM((C_HEADS, 1), F32),
                        pltpu.VMEM((C_HEADS, C_KV_WIDTH), F32),
                        pltpu.VMEM((C_HEADS, 1), F32)])
    return pl.pallas_call(
        _fox_decode_kernel,
        grid_spec=grid_spec,
        out_shape=jax.ShapeDtypeStruct((nb, C_HEADS, HEAD_DIM), BF16),
        compiler_params=_cparams("arbitrary", "arbitrary"),
        name="fox_decode",
    )(page_table, q3, k_new, v_new, lf_new, g3, cache_k, cache_v, cache_lf)


def _split_cols(w, sizes):
    offs = np.cumsum(sizes)[:-1].tolist()
    return jnp.split(w, offs, axis=-1)


def _pad_lanes(a):
    return jnp.pad(a, ((0, 0), (0, LANES - a.shape[-1])))


def _rope_tables(pos):
    half = HEAD_DIM // 2
    inv_freq = ROPE_THETA ** (-jnp.arange(half, dtype=F32) / half)
    ang = pos.astype(F32)[:, None] * inv_freq[None, :]
    cos, sin = jnp.cos(ang), jnp.sin(ang)
    reps = LANES // HEAD_DIM
    return (jnp.tile(jnp.concatenate([cos, cos], axis=1), (1, reps)),
            jnp.tile(jnp.concatenate([-sin, sin], axis=1), (1, reps)))


def _even_weights(w_in, conv_w, conv_b, dt_bias, a_log, ssm_norm_w, w_out, ln_g, ln_b):
    wq, wk, wv, wg, wz, wx, wdt = _split_cols(w_in, EVEN_SPLITS)
    proj = [a.astype(BF16) for a in (wq, wk, wv, wg, wz, wx, _pad_lanes(wdt))]
    wo = w_out.astype(BF16)
    return dict(proj=proj, conv_w=conv_w, conv_b=conv_b[None], dt_bias=_pad_lanes(dt_bias[None]),
                a_log=_pad_lanes(a_log[None]), norm_w=ssm_norm_w[None],
                wo=(wo[:A_WIDTH], wo[A_WIDTH:]), ln_g=ln_g[None], ln_b=ln_b[None])


def _odd_weights(w_in, f_bias, w_out, ln_g, ln_b):
    wq, wk, wv, wf, wg = _split_cols(w_in, ODD_SPLITS)
    proj = [a.astype(BF16) for a in (wq * ATTN_SCALE, wk, wv, _pad_lanes(wf), wg)]
    return dict(proj=proj, f_bias=_pad_lanes(f_bias[None]), wo=(w_out.astype(BF16),),
                ln_g=ln_g[None], ln_b=ln_b[None])


def _even_prompt(x, ew, sinks, d_skip):
    b, t, _ = x.shape
    x2 = x.reshape(b * t, D_MODEL)
    cos, sin = _rope_tables(jnp.arange(t, dtype=jnp.int32))
    q, k, v, g, z, xbc, dt = _even_inproj(x2, cos, sin, ew["proj"], tm=512, act_dtype=BF16)
    o_a = _swa_prompt(q, k, v, g, sinks, b, t)
    o_b, state = _ssd_prompt(xbc, dt, z, ew["conv_w"], ew["conv_b"], ew["dt_bias"], ew["a_log"], d_skip,
                             ew["norm_w"], b, t)
    y = _outproj_ln(x2, (o_a, o_b), ew["wo"], ew["ln_g"], ew["ln_b"], tm=512)
    new_k = k.reshape(b, t, A_KV_WIDTH)[:, -WINDOW:].reshape(b, WINDOW, A_KV_HEADS, HEAD_DIM)
    new_v = v.reshape(b, t, A_KV_WIDTH)[:, -WINDOW:].reshape(b, WINDOW, A_KV_HEADS, HEAD_DIM)
    new_conv = xbc.reshape(b, t, CONV_DIM)[:, -(CONV_WIDTH - 1):]
    return y.reshape(b, t, D_MODEL), new_k, new_v, new_conv, state


def _even_sample(x, pos, ew, sinks, d_skip, swa_k, swa_v, conv_hist, ssm_h0):
    b, t, _ = x.shape
    x2 = x.reshape(b, D_MODEL)
    cos, sin = _rope_tables(jnp.full((b,), pos, jnp.int32))
    q, k, v, g, z, xbc, dt = _even_inproj(x2, cos, sin, ew["proj"], tm=b, act_dtype=F32)
    o_a, new_k, new_v = _swa_sample(
        q.reshape(b, A_HEADS, HEAD_DIM), k.reshape(b, 1, A_KV_WIDTH), v.reshape(b, 1, A_KV_WIDTH),
        swa_k.reshape(b, WINDOW, A_KV_WIDTH), swa_v.reshape(b, WINDOW, A_KV_WIDTH),
        g.reshape(b, A_HEADS, HEAD_DIM), sinks, bb=16)
    o_b, new_conv, new_state = _ssd_sample(xbc.reshape(b, 1, CONV_DIM), conv_hist, dt.reshape(b, 1, LANES),
                                           z.reshape(b, 1, B_WIDTH), ssm_h0,
                                           ew["conv_w"], ew["conv_b"], ew["dt_bias"], ew["a_log"], d_skip,
                                           ew["norm_w"], bb=8)
    y = _outproj_ln(x2, (o_a.reshape(b, A_WIDTH), o_b.reshape(b, B_WIDTH)), ew["wo"], ew["ln_g"], ew["ln_b"],
                    tm=b)
    return (y.reshape(b, 1, D_MODEL), new_k.reshape(b, WINDOW, A_KV_HEADS, HEAD_DIM),
            new_v.reshape(b, WINDOW, A_KV_HEADS, HEAD_DIM), new_conv, new_state)


def _odd_prompt(x, ow):
    b, t, _ = x.shape
    x2 = x.reshape(b * t, D_MODEL)
    q_aug, k_aug, vt_aug, k, v, lf, g = _odd_inproj_prompt(x2, ow["proj"], ow["f_bias"], b, t, tm=256)
    o = _fox_prompt(q_aug, k_aug, vt_aug, g, b, t)
    y = _outproj_ln(x2, (o,), ow["wo"], ow["ln_g"], ow["ln_b"], tm=512)
    return (y.reshape(b, t, D_MODEL), k.reshape(b, t, C_KV_HEADS, HEAD_DIM),
            v.reshape(b, t, C_KV_HEADS, HEAD_DIM), lf.reshape(b, t, C_HEADS))


def _odd_sample(x, ow, cache_k, cache_v, cache_lf, page_table):
    b, t, _ = x.shape
    x2 = x.reshape(b, D_MODEL)
    q, k, v, lf, g = _odd_inproj_sample(x2, ow["proj"], ow["f_bias"])
    n_phys = cache_k.shape[0]
    kt = jnp.transpose(cache_k, (0, 2, 3, 1)).reshape(n_phys, C_KV_WIDTH, PAGE_SIZE)
    vt = jnp.transpose(cache_v, (0, 2, 3, 1)).reshape(n_phys, C_KV_WIDTH, PAGE_SIZE)
    lft = jnp.swapaxes(cache_lf, 1, 2)
    o = _fox_decode(page_table, q.reshape(b, C_HEADS, HEAD_DIM), k.reshape(b, 1, C_KV_WIDTH),
                    v.reshape(b, 1, C_KV_WIDTH), lf.reshape(b, C_HEADS, 1), g.reshape(b, C_HEADS, HEAD_DIM),
                    kt, vt, lft)
    y = _outproj_ln(x2, (o.reshape(b, C_WIDTH),), ow["wo"], ow["ln_g"], ow["ln_b"], tm=b)
    return (y.reshape(b, 1, D_MODEL), k.reshape(b, 1, C_KV_HEADS, HEAD_DIM),
            v.reshape(b, 1, C_KV_HEADS, HEAD_DIM), lf.reshape(b, 1, C_HEADS))


def kernel(x_prompt, x_sample, cache_swa_k, cache_swa_v, state_conv, state_ssm, cache_fox_k, cache_fox_v, cache_fox_logf, page_table, w_in_even, attn_sinks, conv_w, conv_b, dt_bias, a_log, d_skip, ssm_norm_w, w_out_even, ln_g_even, ln_b_even, w_in_odd, forget_bias, w_out_odd, ln_g_odd, ln_b_odd):
    past_len = page_table.shape[1] * PAGE_SIZE
    ew = _even_weights(w_in_even[0], conv_w[0], conv_b[0], dt_bias[0], a_log[0], ssm_norm_w[0],
                       w_out_even[0], ln_g_even[0], ln_b_even[0])
    ow = _odd_weights(w_in_odd[0], forget_bias[0], w_out_odd[0], ln_g_odd[0], ln_b_odd[0])

    yp, swa_kp, swa_vp, conv_p, ssm_p = _even_prompt(x_prompt, ew, attn_sinks[0], d_skip[0])
    ys, swa_ks, swa_vs, conv_s, ssm_s = _even_sample(x_sample, past_len, ew, attn_sinks[0], d_skip[0],
                                                     cache_swa_k[0], cache_swa_v[0], state_conv[0], state_ssm[0])
    yp, fox_kp, fox_vp, fox_lp = _odd_prompt(yp, ow)
    ys, fox_ks, fox_vs, fox_ls = _odd_sample(ys, ow, cache_fox_k[0], cache_fox_v[0], cache_fox_logf[0], page_table)

    one = lambda a: a[None]
    return (yp, ys, one(swa_kp), one(swa_vp), one(swa_ks), one(swa_vs), one(conv_p), one(conv_s),
            one(ssm_p), one(ssm_s), one(fox_kp), one(fox_vp), one(fox_lp), one(fox_ks), one(fox_vs), one(fox_ls))
```

```python
import functools

import numpy as np
import jax
import jax.numpy as jnp
from jax import lax
from jax.experimental import pallas as pl
from jax.experimental.pallas import tpu as pltpu

F32 = jnp.float32
BF16 = jnp.bfloat16

D_MODEL = 1024
DEPTH = 2
HEAD_DIM = 64
ATTN_SCALE = HEAD_DIM ** -0.5
A_HEADS, A_KV_HEADS = 8, 2
A_REP = A_HEADS // A_KV_HEADS
A_WIDTH, A_KV_WIDTH = A_HEADS * HEAD_DIM, A_KV_HEADS * HEAD_DIM
WINDOW = 128
ROPE_THETA = 10000.0
B_HEADS, B_HEAD_DIM, B_GROUPS = 8, 64, 2
B_REP = B_HEADS // B_GROUPS
B_WIDTH = B_HEADS * B_HEAD_DIM
D_STATE = 128
CONV_WIDTH = 4
CONV_DIM = B_WIDTH + 2 * B_GROUPS * D_STATE
SSD_CHUNK = 128
C_HEADS, C_KV_HEADS = 16, 4
C_REP = C_HEADS // C_KV_HEADS
C_WIDTH, C_KV_WIDTH = C_HEADS * HEAD_DIM, C_KV_HEADS * HEAD_DIM
PAGE_SIZE = 128
DN_ALPHA = (2 * DEPTH) ** 0.25
LN_EPS = 1e-5
RMS_EPS = 1e-5
EVEN_SPLITS = (A_WIDTH, A_KV_WIDTH, A_KV_WIDTH, A_WIDTH, B_WIDTH, CONV_DIM, B_HEADS)
ODD_SPLITS = (C_WIDTH, C_KV_WIDTH, C_KV_WIDTH, C_HEADS, C_WIDTH)

LANES = 128
VMEM_LIMIT = 56 * 1024 * 1024
C_PIECES = 3
FOX_TQ = 128
LOG2E = 1.4426950408889634
FOX_TK = 512
DEC_PAGES = 16


def _cparams(*sem):
    return pltpu.CompilerParams(dimension_semantics=sem, vmem_limit_bytes=VMEM_LIMIT)


def _const_spec(shape):
    nd = len(shape)
    return pl.BlockSpec(shape, lambda *_: (0,) * nd)


def _silu(x):
    return x * (1.0 / (1.0 + jnp.exp(-x)))


def _softplus(x):
    return jnp.maximum(x, 0.0) + jnp.log1p(jnp.exp(-jnp.abs(x)))


def _log_sigmoid(x):
    return jnp.minimum(x, 0.0) - jnp.log1p(jnp.exp(-jnp.abs(x)))


def _lane_iota(shape):
    return lax.broadcasted_iota(jnp.int32, shape, len(shape) - 1)


def _dot(a, b):
    return jnp.dot(a, b, preferred_element_type=F32)


def _dot_nt(a, b):
    return lax.dot_general(a, b, (((1,), (1,)), ((), ())), preferred_element_type=F32)


def _dot_tn(a, b):
    return lax.dot_general(a, b, (((0,), (0,)), ((), ())), preferred_element_type=F32)


def _split_heads_to_lane_tiles(x, n_pairs):
    out = []
    for j in range(n_pairs):
        col = x[:, j * LANES:(j + 1) * LANES]
        low = _lane_iota(col.shape) < HEAD_DIM
        out.append(jnp.where(low, col, 0.0))
        out.append(jnp.where(low, pltpu.roll(col, HEAD_DIM, axis=1), 0.0))
    return out


def _merge_heads_from_lane_tiles(tiles):
    cols = []
    for j in range(len(tiles) // 2):
        a, b = tiles[2 * j], tiles[2 * j + 1]
        low = _lane_iota(a.shape) < HEAD_DIM
        cols.append(jnp.where(low, a, pltpu.roll(b, HEAD_DIM, axis=1)))
    return jnp.concatenate(cols, axis=1) if len(cols) > 1 else cols[0]


def _rope_cols(x, cos, sin_signed):
    half = HEAD_DIM // 2
    first = (_lane_iota(x.shape) % HEAD_DIM) < half
    swapped = jnp.where(first, pltpu.roll(x, LANES - half, axis=1), pltpu.roll(x, half, axis=1))
    return x * cos + swapped * sin_signed


def _even_inproj_kernel(x_ref, cos_ref, sin_ref, wq, wk, wv, wg, wz, wx, wdt,
                        q_o, k_o, v_o, g_o, z_o, xbc_o, dt_o):
    xb = x_ref[...].astype(BF16)
    cos, sin = cos_ref[...], sin_ref[...]
    q = _dot(xb, wq[...])
    for j in range(A_WIDTH // LANES):
        sl = slice(j * LANES, (j + 1) * LANES)
        q_o[:, sl] = (_rope_cols(q[:, sl], cos, sin) * ATTN_SCALE).astype(q_o.dtype)
    k_o[...] = _rope_cols(_dot(xb, wk[...]), cos, sin)
    v_o[...] = _dot(xb, wv[...])
    g_o[...] = _dot(xb, wg[...]).astype(g_o.dtype)
    z_o[...] = _dot(xb, wz[...]).astype(z_o.dtype)
    xbc_o[...] = _dot(xb, wx[...])
    dt_o[...] = _dot(xb, wdt[...])


def _even_inproj(x2d, cos_tab, sin_tab, w, tm, act_dtype):
    n = x2d.shape[0]
    nblk_seq = cos_tab.shape[0] // tm
    row = lambda i: (i, 0)
    outs = [((n, A_WIDTH), act_dtype), ((n, A_KV_WIDTH), F32), ((n, A_KV_WIDTH), F32),
            ((n, A_WIDTH), act_dtype), ((n, B_WIDTH), act_dtype), ((n, CONV_DIM), F32), ((n, LANES), F32)]
    return pl.pallas_call(
        _even_inproj_kernel,
        grid=(n // tm,),
        in_specs=[pl.BlockSpec((tm, D_MODEL), row),
                  pl.BlockSpec((tm, LANES), lambda i: (i % nblk_seq, 0)),
                  pl.BlockSpec((tm, LANES), lambda i: (i % nblk_seq, 0))]
                 + [_const_spec(a.shape) for a in w],
        out_specs=[pl.BlockSpec((tm, s[1]), row) for s, _ in outs],
        out_shape=[jax.ShapeDtypeStruct(s, d) for s, d in outs],
        compiler_params=_cparams("parallel"),
        name="even_inproj",
    )(x2d, cos_tab, sin_tab, *w)


def _swa_prompt_kernel(sinks_ref, q_ref, kp_ref, kc_ref, vp_ref, vc_ref, g_ref, o_ref):
    n = pl.program_id(1)
    q = q_ref[...]
    kk = jnp.concatenate([kp_ref[...], kc_ref[...]], axis=0).astype(BF16)
    vv = jnp.concatenate([vp_ref[...], vc_ref[...]], axis=0).astype(BF16)
    qpos = WINDOW + lax.broadcasted_iota(jnp.int32, (WINDOW, 2 * WINDOW), 0)
    kpos = lax.broadcasted_iota(jnp.int32, (WINDOW, 2 * WINDOW), 1)
    diff = qpos - kpos
    mask = (diff >= 0) & (diff <= WINDOW) & ((kpos >= WINDOW) | (n > 0))
    heads = []
    for h in range(A_HEADS):
        kv = h // A_REP
        qh = q[:, h * HEAD_DIM:(h + 1) * HEAD_DIM]
        kh = kk[:, kv * HEAD_DIM:(kv + 1) * HEAD_DIM]
        vh = vv[:, kv * HEAD_DIM:(kv + 1) * HEAD_DIM]
        s = jnp.where(mask, _dot_nt(qh, kh), -jnp.inf)
        sink = sinks_ref[h]
        m = jnp.maximum(jnp.max(s, axis=-1, keepdims=True), sink)
        p = jnp.exp(s - m)
        den = jnp.sum(p, axis=-1, keepdims=True) + jnp.exp(sink - m)
        heads.append(_dot((p / den).astype(BF16), vh))
    o = jnp.concatenate(heads, axis=1)
    o_ref[...] = (o * _silu(g_ref[...].astype(F32))).astype(o_ref.dtype)


def _swa_prompt(q, k, v, g, sinks, batch, seq):
    nb = seq // WINDOW
    cur = lambda b, n: (b * nb + n, 0)
    prev = lambda b, n: (b * nb + jnp.maximum(n - 1, 0), 0)
    return pl.pallas_call(
        _swa_prompt_kernel,
        grid=(batch, nb),
        in_specs=[pl.BlockSpec(memory_space=pltpu.SMEM),
                  pl.BlockSpec((WINDOW, A_WIDTH), cur),
                  pl.BlockSpec((WINDOW, A_KV_WIDTH), prev),
                  pl.BlockSpec((WINDOW, A_KV_WIDTH), cur),
                  pl.BlockSpec((WINDOW, A_KV_WIDTH), prev),
                  pl.BlockSpec((WINDOW, A_KV_WIDTH), cur),
                  pl.BlockSpec((WINDOW, A_WIDTH), cur)],
        out_specs=pl.BlockSpec((WINDOW, A_WIDTH), cur),
        out_shape=jax.ShapeDtypeStruct((batch * seq, A_WIDTH), BF16),
        compiler_params=_cparams("parallel", "parallel"),
        name="swa_prompt",
    )(sinks, q, k, k, v, v, g)


def _swa_sample_kernel(sinks_ref, q_ref, kn_ref, vn_ref, ck_ref, cv_ref, g_ref, o_ref, nk_ref, nv_ref):
    ck, cv = ck_ref[...], cv_ref[...]
    kn, vn = kn_ref[...], vn_ref[...]
    nk_ref[:, 0:WINDOW - 1, :] = ck[:, 1:WINDOW, :]
    nk_ref[:, WINDOW - 1:WINDOW, :] = kn
    nv_ref[:, 0:WINDOW - 1, :] = cv[:, 1:WINDOW, :]
    nv_ref[:, WINDOW - 1:WINDOW, :] = vn
    q = q_ref[...]
    g = g_ref[...]
    hrow = lax.broadcasted_iota(jnp.int32, (A_REP, 1), 0)
    for kv in range(A_KV_HEADS):
        hs = slice(kv * A_REP, (kv + 1) * A_REP)
        ds = slice(kv * HEAD_DIM, (kv + 1) * HEAD_DIM)
        qh = q[:, hs, :].astype(BF16)
        kh, vh = ck[:, :, ds].astype(BF16), cv[:, :, ds].astype(BF16)
        knh, vnh = kn[:, :, ds].astype(BF16).astype(F32), vn[:, :, ds].astype(BF16).astype(F32)
        s_c = jnp.einsum("bqd,bkd->bqk", qh, kh, preferred_element_type=F32)
        s_n = jnp.sum(qh.astype(F32) * knh, axis=-1, keepdims=True)
        sink = jnp.zeros((A_REP, 1), F32)
        for r in range(A_REP):
            sink = jnp.where(hrow == r, sinks_ref[kv * A_REP + r], sink)
        sink = sink[None]
        m = jnp.maximum(jnp.maximum(jnp.max(s_c, axis=-1, keepdims=True), s_n), sink)
        p_c = jnp.exp(s_c - m)
        p_n = jnp.exp(s_n - m)
        den = jnp.sum(p_c, axis=-1, keepdims=True) + p_n + jnp.exp(sink - m)
        p_c = (p_c / den).astype(BF16)
        p_n = (p_n / den).astype(BF16).astype(F32)
        o = jnp.einsum("bqk,bkd->bqd", p_c, vh, preferred_element_type=F32) + p_n * vnh
        o_ref[:, hs, :] = (o * _silu(g[:, hs, :])).astype(o_ref.dtype)


def _swa_sample(q3, k_new, v_new, cache_k, cache_v, g3, sinks, bb):
    nb = q3.shape[0]
    blk3 = lambda i: (i, 0, 0)
    return pl.pallas_call(
        _swa_sample_kernel,
        grid=(nb // bb,),
        in_specs=[pl.BlockSpec(memory_space=pltpu.SMEM),
                  pl.BlockSpec((bb, A_HEADS, HEAD_DIM), blk3),
                  pl.BlockSpec((bb, 1, A_KV_WIDTH), blk3),
                  pl.BlockSpec((bb, 1, A_KV_WIDTH), blk3),
                  pl.BlockSpec((bb, WINDOW, A_KV_WIDTH), blk3),
                  pl.BlockSpec((bb, WINDOW, A_KV_WIDTH), blk3),
                  pl.BlockSpec((bb, A_HEADS, HEAD_DIM), blk3)],
        out_specs=[pl.BlockSpec((bb, A_HEADS, HEAD_DIM), blk3),
                   pl.BlockSpec((bb, WINDOW, A_KV_WIDTH), blk3),
                   pl.BlockSpec((bb, WINDOW, A_KV_WIDTH), blk3)],
        out_shape=[jax.ShapeDtypeStruct((nb, A_HEADS, HEAD_DIM), F32),
                   jax.ShapeDtypeStruct((nb, WINDOW, A_KV_WIDTH), F32),
                   jax.ShapeDtypeStruct((nb, WINDOW, A_KV_WIDTH), F32)],
        compiler_params=_cparams("parallel"),
        name="swa_sample",
    )(sinks, q3, k_new, v_new, cache_k, cache_v, g3)


def _gated_rmsnorm(y, z, w):
    g = y * _silu(z)
    gw = B_WIDTH // B_GROUPS
    outs = []
    for i in range(B_GROUPS):
        gi = g[..., i * gw:(i + 1) * gw]
        outs.append(gi * lax.rsqrt(jnp.mean(gi * gi, axis=-1, keepdims=True) + RMS_EPS))
    return jnp.concatenate(outs, axis=-1) * w


def _ssd_prompt_kernel(dskip_ref, xbc_ref, dt_ref, z_ref, cw_ref, cb_ref, dtb_ref, alog_ref, nw_ref,
                       o_ref, state_ref, xp_ref):
    L = SSD_CHUNK
    c = pl.program_id(1)

    @pl.when(c == 0)
    def _():
        state_ref[...] = jnp.zeros_like(state_ref)
        xp_ref[0:8, :] = jnp.zeros((8, CONV_DIM), F32)

    x = xbc_ref[...]
    xp_ref[8:8 + L, :] = x
    acc = cb_ref[...] + xp_ref[pl.ds(8 - (CONV_WIDTH - 1), L), :] * cw_ref[0:1, :]
    for j in range(1, CONV_WIDTH):
        acc = acc + xp_ref[pl.ds(8 - (CONV_WIDTH - 1) + j, L), :] * cw_ref[j:j + 1, :]
    xp_ref[0:8, :] = x[L - 8:L, :]
    u = _silu(acc)
    xs = u[:, :B_WIDTH]
    bm = u[:, B_WIDTH:B_WIDTH + B_GROUPS * D_STATE].astype(BF16)
    cm = u[:, B_WIDTH + B_GROUPS * D_STATE:].astype(BF16)

    dt = _softplus(dt_ref[...] + dtb_ref[...])
    da = dt * (-jnp.exp(alog_ref[...]))
    row = lax.broadcasted_iota(jnp.int32, (L, L), 0)
    col = lax.broadcasted_iota(jnp.int32, (L, L), 1)
    causal = row >= col
    cum = jnp.dot(causal.astype(F32), da, precision=lax.Precision.HIGHEST, preferred_element_type=F32)
    cum_t = cum.T
    cb = [_dot_nt(cm[:, g * D_STATE:(g + 1) * D_STATE], bm[:, g * D_STATE:(g + 1) * D_STATE])
          for g in range(B_GROUPS)]

    ys = []
    for h in range(B_HEADS):
        g = h // B_REP
        cum_col = cum[:, h:h + 1]
        cum_row = cum_t[h:h + 1, :]
        cum_end = cum[L - 1:L, h:h + 1]
        xs_h = xs[:, h * B_HEAD_DIM:(h + 1) * B_HEAD_DIM]
        xdt = xs_h * dt[:, h:h + 1]
        decay = jnp.exp(jnp.where(causal, cum_col - cum_row, -jnp.inf))
        y = _dot((cb[g] * decay).astype(BF16), xdt.astype(BF16))
        st = state_ref[0, h]
        cm_g = cm[:, g * D_STATE:(g + 1) * D_STATE]
        y = y + _dot_nt(cm_g, st.astype(BF16)) * jnp.exp(cum_col)
        ys.append(y + dskip_ref[h] * xs_h)
        xw = (xdt * jnp.exp(cum_end - cum_col)).astype(BF16)
        state_ref[0, h] = st * jnp.exp(cum_end) + _dot_tn(xw, bm[:, g * D_STATE:(g + 1) * D_STATE])
    y = jnp.concatenate(ys, axis=1)
    o_ref[...] = _gated_rmsnorm(y, z_ref[...].astype(F32), nw_ref[...]).astype(o_ref.dtype)


def _ssd_prompt(xbc, dt, z, conv_w, conv_b, dt_bias, a_log, d_skip, norm_w, batch, seq):
    nc = seq // SSD_CHUNK
    cur = lambda b, c: (b * nc + c, 0)
    return pl.pallas_call(
        _ssd_prompt_kernel,
        grid=(batch, nc),
        in_specs=[pl.BlockSpec(memory_space=pltpu.SMEM),
                  pl.BlockSpec((SSD_CHUNK, CONV_DIM), cur),
                  pl.BlockSpec((SSD_CHUNK, LANES), cur),
                  pl.BlockSpec((SSD_CHUNK, B_WIDTH), cur),
                  _const_spec(conv_w.shape), _const_spec(conv_b.shape), _const_spec(dt_bias.shape),
                  _const_spec(a_log.shape), _const_spec(norm_w.shape)],
        out_specs=[pl.BlockSpec((SSD_CHUNK, B_WIDTH), cur),
                   pl.BlockSpec((1, B_HEADS, B_HEAD_DIM, D_STATE), lambda b, c: (b, 0, 0, 0))],
        out_shape=[jax.ShapeDtypeStruct((batch * seq, B_WIDTH), BF16),
                   jax.ShapeDtypeStruct((batch, B_HEADS, B_HEAD_DIM, D_STATE), F32)],
        scratch_shapes=[pltpu.VMEM((8 + SSD_CHUNK, CONV_DIM), F32)],
        compiler_params=_cparams("parallel", "arbitrary"),
        name="ssd_prompt",
    )(d_skip, xbc, dt, z, conv_w, conv_b, dt_bias, a_log, norm_w)


def _ssd_sample_kernel(dskip_ref, xbc_ref, hist_ref, dt_ref, z_ref, h0_ref, cw_ref, cb_ref, dtb_ref,
                       alog_ref, nw_ref, o_ref, conv_o, state_o):
    x = xbc_ref[...]
    hist = hist_ref[...]
    acc = cb_ref[...][None] + x * cw_ref[CONV_WIDTH - 1:CONV_WIDTH, :][None]
    for j in range(CONV_WIDTH - 1):
        acc = acc + hist[:, j:j + 1, :] * cw_ref[j:j + 1, :][None]
    conv_o[:, 0:CONV_WIDTH - 2, :] = hist[:, 1:CONV_WIDTH - 1, :]
    conv_o[:, CONV_WIDTH - 2:CONV_WIDTH - 1, :] = x
    u = _silu(acc)
    xs = u[:, :, :B_WIDTH]
    bm = u[:, :, B_WIDTH:B_WIDTH + B_GROUPS * D_STATE]
    cm = u[:, :, B_WIDTH + B_GROUPS * D_STATE:]
    dt = _softplus(dt_ref[...] + dtb_ref[...][None])
    dec = jnp.exp(dt * (-jnp.exp(alog_ref[...]))[None])
    ys = []
    for h in range(B_HEADS):
        g = h // B_REP
        xs_h = xs[:, :, h * B_HEAD_DIM:(h + 1) * B_HEAD_DIM]
        xdt = xs_h * dt[:, :, h:h + 1]
        bm_g = bm[:, :, g * D_STATE:(g + 1) * D_STATE]
        cm_g = cm[:, :, g * D_STATE:(g + 1) * D_STATE]
        dec_h = dec[:, :, h:h + 1]
        st = h0_ref[:, h]
        cb = jnp.sum(cm_g * bm_g, axis=-1, keepdims=True)
        y_off = jnp.einsum("bqn,bpn->bqp", cm_g.astype(BF16), st.astype(BF16),
                           preferred_element_type=F32)
        ys.append(cb * xdt + y_off * dec_h + dskip_ref[h] * xs_h)
        outer = jnp.einsum("bqp,bqn->bpn", xdt.astype(BF16), bm_g.astype(BF16),
                           preferred_element_type=F32)
        state_o[:, h] = st * dec_h + outer
    y = jnp.concatenate(ys, axis=-1)
    o_ref[...] = _gated_rmsnorm(y, z_ref[...], nw_ref[...][None]).astype(o_ref.dtype)


def _ssd_sample(xbc3, hist, dt3, z3, h0, conv_w, conv_b, dt_bias, a_log, d_skip, norm_w, bb):
    nb = xbc3.shape[0]
    b3 = lambda i: (i, 0, 0)
    b4 = lambda i: (i, 0, 0, 0)
    return pl.pallas_call(
        _ssd_sample_kernel,
        grid=(nb // bb,),
        in_specs=[pl.BlockSpec(memory_space=pltpu.SMEM),
                  pl.BlockSpec((bb, 1, CONV_DIM), b3),
                  pl.BlockSpec((bb, CONV_WIDTH - 1, CONV_DIM), b3),
                  pl.BlockSpec((bb, 1, LANES), b3),
                  pl.BlockSpec((bb, 1, B_WIDTH), b3),
                  pl.BlockSpec((bb, B_HEADS, B_HEAD_DIM, D_STATE), b4),
                  _const_spec(conv_w.shape), _const_spec(conv_b.shape), _const_spec(dt_bias.shape),
                  _const_spec(a_log.shape), _const_spec(norm_w.shape)],
        out_specs=[pl.BlockSpec((bb, 1, B_WIDTH), b3),
                   pl.BlockSpec((bb, CONV_WIDTH - 1, CONV_DIM), b3),
                   pl.BlockSpec((bb, B_HEADS, B_HEAD_DIM, D_STATE), b4)],
        out_shape=[jax.ShapeDtypeStruct((nb, 1, B_WIDTH), F32),
                   jax.ShapeDtypeStruct((nb, CONV_WIDTH - 1, CONV_DIM), F32),
                   jax.ShapeDtypeStruct((nb, B_HEADS, B_HEAD_DIM, D_STATE), F32)],
        compiler_params=_cparams("parallel"),
        name="ssd_sample",
    )(d_skip, xbc3, hist, dt3, z3, h0, conv_w, conv_b, dt_bias, a_log, norm_w)


def _outproj_ln_kernel(n_parts, x_ref, *refs):
    parts, ws = refs[:n_parts], refs[n_parts:2 * n_parts]
    g_ref, b_ref, o_ref = refs[2 * n_parts:]
    mix = _dot(parts[0][...].astype(BF16), ws[0][...])
    for a, w in zip(parts[1:], ws[1:]):
        mix = mix + _dot(a[...].astype(BF16), w[...])
    y = DN_ALPHA * x_ref[...] + mix
    mu = jnp.mean(y, axis=-1, keepdims=True)
    var = jnp.mean(jnp.square(y - mu), axis=-1, keepdims=True)
    o_ref[...] = (y - mu) * lax.rsqrt(var + LN_EPS) * g_ref[...] + b_ref[...]


def _outproj_ln(x2d, parts, ws, ln_g, ln_b, tm):
    n = x2d.shape[0]
    row = lambda i: (i, 0)
    return pl.pallas_call(
        functools.partial(_outproj_ln_kernel, len(parts)),
        grid=(n // tm,),
        in_specs=[pl.BlockSpec((tm, D_MODEL), row)]
                 + [pl.BlockSpec((tm, p.shape[1]), row) for p in parts]
                 + [_const_spec(w.shape) for w in ws]
                 + [_const_spec(ln_g.shape), _const_spec(ln_b.shape)],
        out_specs=pl.BlockSpec((tm, D_MODEL), row),
        out_shape=jax.ShapeDtypeStruct((n, D_MODEL), F32),
        compiler_params=_cparams("parallel"),
        name="outproj_ln",
    )(x2d, *parts, *ws, ln_g, ln_b)


def _bf16_pieces(c):
    pieces, rem = [], c
    for _ in range(C_PIECES):
        p = rem.astype(BF16).astype(F32)
        pieces.append(p)
        rem = rem - p
    return pieces


def _odd_inproj_prompt_kernel(tiles_per_seq, x_ref, wq, wk, wv, wf, wg, fb_ref, sq_ref, sk_ref, cq_ref, ck_ref,
                              qa_o, ka_o, vt_o, k_o, v_o, lf_o, g_o, carry_ref):
    i = pl.program_id(0)
    tm = x_ref.shape[0]

    @pl.when(i % tiles_per_seq == 0)
    def _():
        carry_ref[...] = jnp.zeros_like(carry_ref)

    xb = x_ref[...].astype(BF16)
    kf = _dot(xb, wk[...])
    vf = _dot(xb, wv[...])
    k_o[...] = kf
    v_o[...] = vf
    g_o[...] = _dot(xb, wg[...]).astype(g_o.dtype)
    lf = _log_sigmoid(_dot(xb, wf[...]) + fb_ref[...])
    lf_o[...] = lf[:, :C_HEADS]

    c = jnp.where(_lane_iota(lf.shape) < C_HEADS, lf, 0.0)
    rows = lax.broadcasted_iota(jnp.int32, c.shape, 0)
    s = 1
    while s < tm:
        c = c + jnp.where(rows >= s, pltpu.roll(c, s, axis=0), 0.0)
        s *= 2
    c = c + carry_ref[...]
    carry_ref[...] = c[tm - 1:tm, :]

    hi, mid, lo = _bf16_pieces(c * LOG2E)
    c3 = (hi + pltpu.roll(mid, C_HEADS, axis=1) + pltpu.roll(lo, 2 * C_HEADS, axis=1)).astype(BF16)
    ex_q = _dot(c3, sq_ref[...]) + cq_ref[...]
    ex_k = _dot(c3, sk_ref[...]) + ck_ref[...]

    qf = _dot(xb, wq[...])
    for h, t in enumerate(_split_heads_to_lane_tiles(qf, C_WIDTH // LANES)):
        sl = slice(h * LANES, (h + 1) * LANES)
        qa_o[:, sl] = (t + ex_q[:, sl]).astype(qa_o.dtype)
    ones_row = (_lane_iota((1, LANES)) == HEAD_DIM).astype(F32)
    k_tiles = _split_heads_to_lane_tiles(kf, C_KV_WIDTH // LANES)
    v_tiles = _split_heads_to_lane_tiles(vf, C_KV_WIDTH // LANES)
    for g in range(C_KV_HEADS):
        sl = slice(g * LANES, (g + 1) * LANES)
        ka_o[:, sl] = (k_tiles[g] + ex_k[:, sl]).astype(ka_o.dtype)
        vt_o[0, g] = (v_tiles[g] + ones_row).T.astype(vt_o.dtype)


def _aug_selectors():
    sq = np.zeros((LANES, C_HEADS * LANES), np.float32)
    sk = np.zeros((LANES, C_KV_HEADS * LANES), np.float32)
    cq = np.zeros((1, C_HEADS * LANES), np.float32)
    ck = np.zeros((1, C_KV_HEADS * LANES), np.float32)
    for h in range(C_HEADS):
        g, r = divmod(h, C_REP)
        for p in range(C_PIECES):
            sq[p * C_HEADS + h, h * LANES + HEAD_DIM + p] = 1.0
            col = HEAD_DIM + C_PIECES + C_PIECES * r + p
            cq[0, h * LANES + col] = -1.0
            sk[p * C_HEADS + h, g * LANES + col] = 1.0
    for g in range(C_KV_HEADS):
        for p in range(C_PIECES):
            ck[0, g * LANES + HEAD_DIM + p] = 1.0
    return jnp.asarray(sq, BF16), jnp.asarray(sk, BF16), jnp.asarray(cq), jnp.asarray(ck)


def _odd_inproj_prompt(x2d, w, f_bias, batch, seq, tm):
    n = x2d.shape[0]
    tiles = seq // tm
    sq, sk, cq, ck = _aug_selectors()
    row = lambda i: (i, 0)
    consts = list(w) + [f_bias, sq, sk, cq, ck]
    outs = [((n, C_HEADS * LANES), BF16), ((n, C_KV_HEADS * LANES), BF16),
            ((batch, C_KV_HEADS, LANES, seq), BF16), ((n, C_KV_WIDTH), F32), ((n, C_KV_WIDTH), F32),
            ((n, C_HEADS), F32), ((n, C_WIDTH), BF16)]
    out_specs = [pl.BlockSpec((tm, outs[0][0][1]), row), pl.BlockSpec((tm, outs[1][0][1]), row),
                 pl.BlockSpec((1, C_KV_HEADS, LANES, tm), lambda i: (i // tiles, 0, 0, i % tiles)),
                 pl.BlockSpec((tm, C_KV_WIDTH), row), pl.BlockSpec((tm, C_KV_WIDTH), row),
                 pl.BlockSpec((tm, C_HEADS), row), pl.BlockSpec((tm, C_WIDTH), row)]
    return pl.pallas_call(
        functools.partial(_odd_inproj_prompt_kernel, tiles),
        grid=(n // tm,),
        in_specs=[pl.BlockSpec((tm, D_MODEL), row)] + [_const_spec(a.shape) for a in consts],
        out_specs=out_specs,
        out_shape=[jax.ShapeDtypeStruct(s, d) for s, d in outs],
        scratch_shapes=[pltpu.VMEM((1, LANES), F32)],
        compiler_params=_cparams("arbitrary"),
        name="odd_inproj_prompt",
    )(x2d, *consts)


def _odd_inproj_sample_kernel(x_ref, wq, wk, wv, wf, wg, fb_ref, q_o, k_o, v_o, lf_o, g_o):
    xb = x_ref[...].astype(BF16)
    q_o[...] = _dot(xb, wq[...])
    k_o[...] = _dot(xb, wk[...])
    v_o[...] = _dot(xb, wv[...])
    lf_o[...] = _log_sigmoid(_dot(xb, wf[...]) + fb_ref[...])[:, :C_HEADS]
    g_o[...] = _dot(xb, wg[...])


def _odd_inproj_sample(x2d, w, f_bias):
    n = x2d.shape[0]
    outs = [((n, C_WIDTH), F32), ((n, C_KV_WIDTH), F32), ((n, C_KV_WIDTH), F32), ((n, C_HEADS), F32),
            ((n, C_WIDTH), F32)]
    return pl.pallas_call(
        _odd_inproj_sample_kernel,
        out_shape=[jax.ShapeDtypeStruct(s, d) for s, d in outs],
        compiler_params=pltpu.CompilerParams(vmem_limit_bytes=VMEM_LIMIT),
        name="odd_inproj_sample",
    )(x2d, *w, f_bias)


def _fox_prompt_kernel(q_ref, k_ref, vt_ref, g_ref, o_ref, qs_ref, m_ref, alpha_ref, acc_ref, p_ref, s_ref):
    qi = pl.program_id(1)
    tq, tk = FOX_TQ, FOX_TK
    for g in range(C_KV_HEADS):
        for r in range(C_REP):
            h = g * C_REP + r
            qs_ref[g, r * tq:(r + 1) * tq, :] = q_ref[:, h * LANES:(h + 1) * LANES]
    m_ref[...] = jnp.full(m_ref.shape, -jnp.inf, F32)
    acc_ref[...] = jnp.zeros_like(acc_ref)
    n_last = (qi * tq) // tk

    def qk(t, slot, masked):
        start = t * tk if isinstance(t, int) else pl.multiple_of(t * tk, tk)
        for g in range(C_KV_HEADS):
            k_t = k_ref[pl.ds(start, tk), g * LANES:(g + 1) * LANES]
            s_t = _dot_nt(k_t, qs_ref[g])
            if masked:
                krow = lax.broadcasted_iota(jnp.int32, (tk, 1), 0)
                qrel = (qi * tq - start) + _lane_iota((1, C_REP * tq)) % tq
                s_t = jnp.where(krow <= qrel, s_t, -jnp.inf)
            s_ref[slot, g] = s_t

    def softmax(slot):
        for g in range(C_KV_HEADS):
            s_t = s_ref[slot, g]
            m_old = m_ref[g]
            m_new = jnp.maximum(m_old, jnp.max(s_t, axis=0, keepdims=True))
            p_ref[slot, g] = jnp.exp2(s_t - m_new).astype(BF16)
            alpha_ref[slot, g] = jnp.exp2(m_old - m_new)
            m_ref[g] = m_new

    def pv(t, slot):
        start = t * tk if isinstance(t, int) else pl.multiple_of(t * tk, tk)
        for g in range(C_KV_HEADS):
            acc_ref[g] = (acc_ref[g] * alpha_ref[slot, g]
                          + _dot(vt_ref[0, g, :, pl.ds(start, tk)], p_ref[slot, g]))

    def guarded_step(tau, parity):
        @pl.when(tau <= n_last)
        def _():
            qk(tau, parity, True)

        @pl.when((tau >= 1) & (tau <= n_last + 1))
        def _():
            softmax(1 - parity)

        @pl.when((tau >= 2) & (tau <= n_last + 2))
        def _():
            pv(tau - 2, parity)

    def steady_pair(i, carry):
        tau = 2 + 2 * i
        qk(tau, 0, False)
        softmax(1)
        pv(tau - 2, 0)
        qk(tau + 1, 1, False)
        softmax(0)
        pv(tau - 1, 1)
        return carry

    @pl.when(n_last < 3)
    def _():
        for tau in range(6):
            guarded_step(tau, tau % 2)

    @pl.when(n_last >= 3)
    def _():
        qk(0, 0, False)
        qk(1, 1, False)
        softmax(0)
        n_pairs = (n_last - 2) // 2
        lax.fori_loop(0, n_pairs, steady_pair, 0)

        @pl.when(n_last % 2 == 0)
        def _():
            qk(n_last, 0, True)
            softmax(1)
            pv(n_last - 2, 0)
            softmax(0)
            pv(n_last - 1, 1)
            pv(n_last, 0)

        @pl.when(n_last % 2 == 1)
        def _():
            qk(n_last - 1, 0, False)
            softmax(1)
            pv(n_last - 3, 0)
            qk(n_last, 1, True)
            softmax(0)
            pv(n_last - 2, 1)
            softmax(1)
            pv(n_last - 1, 0)
            pv(n_last, 1)

    tiles = []
    for g in range(C_KV_HEADS):
        acc = acc_ref[g]
        for r in range(C_REP):
            t = acc[:, r * tq:(r + 1) * tq].T
            tiles.append(t / t[:, HEAD_DIM:HEAD_DIM + 1])
    o = _merge_heads_from_lane_tiles(tiles)
    o_ref[...] = (o * _silu(g_ref[...].astype(F32))).astype(o_ref.dtype)


def _fox_prompt(q_aug, k_aug, vt_aug, g, batch, seq):
    nq = seq // FOX_TQ
    row = lambda b, i: (b * nq + i, 0)
    once = pl.Buffered(1)
    return pl.pallas_call(
        _fox_prompt_kernel,
        grid=(batch, nq),
        in_specs=[pl.BlockSpec((FOX_TQ, C_HEADS * LANES), row),
                  pl.BlockSpec((seq, C_KV_HEADS * LANES), lambda b, i: (b, 0), pipeline_mode=once),
                  pl.BlockSpec((1, C_KV_HEADS, LANES, seq), lambda b, i: (b, 0, 0, 0), pipeline_mode=once),
                  pl.BlockSpec((FOX_TQ, C_WIDTH), row)],
        out_specs=pl.BlockSpec((FOX_TQ, C_WIDTH), row),
        out_shape=jax.ShapeDtypeStruct((batch * seq, C_WIDTH), BF16),
        scratch_shapes=[pltpu.VMEM((C_KV_HEADS, C_REP * FOX_TQ, LANES), BF16),
                        pltpu.VMEM((C_KV_HEADS, 1, C_REP * FOX_TQ), F32),
                        pltpu.VMEM((2, C_KV_HEADS, 1, C_REP * FOX_TQ), F32),
                        pltpu.VMEM((C_KV_HEADS, LANES, C_REP * FOX_TQ), F32),
                        pltpu.VMEM((2, C_KV_HEADS, FOX_TK, C_REP * FOX_TQ), BF16),
                        pltpu.VMEM((2, C_KV_HEADS, FOX_TK, C_REP * FOX_TQ), F32)],
        compiler_params=_cparams("parallel", "arbitrary"),
        name="fox_prompt",
    )(q_aug, k_aug, vt_aug, g)


def _fox_decode_kernel(pt_ref, q_ref, kn_ref, vn_ref, lfn_ref, g_ref, k_hbm, v_hbm, lf_hbm, o_ref,
                       kbuf, vbuf, lfbuf, sem, qm_ref, m_ref, l_ref, acc_ref, carry_ref):
    b, c = pl.program_id(0), pl.program_id(1)
    nb, nc = pl.num_programs(0), pl.num_programs(1)
    G = DEC_PAGES
    n_pages = nc * G
    step = b * nc + c
    slot = step % 2

    def copies(bb, cc, sl):
        out = []
        for g in range(G):
            page = pt_ref[bb, n_pages - (cc + 1) * G + g]
            out.append(pltpu.make_async_copy(k_hbm.at[page], kbuf.at[sl, g], sem.at[0, sl]))
            out.append(pltpu.make_async_copy(v_hbm.at[page], vbuf.at[sl, g], sem.at[1, sl]))
            out.append(pltpu.make_async_copy(lf_hbm.at[page], lfbuf.at[sl, g], sem.at[2, sl]))
        return out

    @pl.when(step == 0)
    def _():
        for cp in copies(0, 0, 0):
            cp.start()

    @pl.when(step + 1 < nb * nc)
    def _():
        nxt = step + 1
        for cp in copies(nxt // nc, nxt % nc, 1 - slot):
            cp.start()

    @pl.when(c == 0)
    def _():
        q16 = q_ref[0]
        q4 = jnp.concatenate([q16] * C_KV_HEADS, axis=1)
        hrow = lax.broadcasted_iota(jnp.int32, q4.shape, 0) // C_REP
        gcol = _lane_iota(q4.shape) // HEAD_DIM
        qm = jnp.where(hrow == gcol, q4, 0.0).astype(BF16)
        qm_ref[...] = qm
        kn = kn_ref[0].astype(BF16).astype(F32)
        m_ref[...] = jnp.sum(qm.astype(F32) * kn, axis=-1, keepdims=True)
        l_ref[...] = jnp.ones_like(l_ref)
        acc_ref[...] = jnp.broadcast_to(vn_ref[0].astype(BF16).astype(F32), acc_ref.shape)
        carry_ref[...] = lfn_ref[0]

    for cp in copies(b, c, slot):
        cp.wait()

    lf_rows = lfbuf[slot].reshape(G * C_HEADS, PAGE_SIZE)
    hi = lf_rows.astype(BF16)
    lo = (lf_rows - hi.astype(F32)).astype(BF16)
    ii = lax.broadcasted_iota(jnp.int32, (PAGE_SIZE, PAGE_SIZE), 0)
    jj = lax.broadcasted_iota(jnp.int32, (PAGE_SIZE, PAGE_SIZE), 1)
    later = (ii > jj).astype(BF16)
    within = _dot(hi, later) + _dot(lo, later)
    carry = carry_ref[...]
    cols = [None] * G
    for g in reversed(range(G)):
        rs = slice(g * C_HEADS, (g + 1) * C_HEADS)
        cols[g] = within[rs] + carry
        carry = carry + within[rs][:, 0:1] + lf_rows[rs][:, 0:1]
    carry_ref[...] = carry

    kc = jnp.concatenate([kbuf[slot, g].astype(BF16) for g in range(G)], axis=1)
    s = _dot(qm_ref[...], kc) + jnp.concatenate(cols, axis=1)
    m_old = m_ref[...]
    m_new = jnp.maximum(m_old, jnp.max(s, axis=-1, keepdims=True))
    alpha = jnp.exp(m_old - m_new)
    p = jnp.exp(s - m_new)
    l_ref[...] = l_ref[...] * alpha + jnp.sum(p, axis=-1, keepdims=True)
    vc = jnp.concatenate([vbuf[slot, g].astype(BF16) for g in range(G)], axis=1)
    acc_ref[...] = acc_ref[...] * alpha + _dot_nt(p.astype(BF16), vc)
    m_ref[...] = m_new

    @pl.when(c == nc - 1)
    def _():
        o_all = acc_ref[...] / l_ref[...]
        hrow = lax.broadcasted_iota(jnp.int32, (C_HEADS, HEAD_DIM), 0) // C_REP
        o = jnp.zeros((C_HEADS, HEAD_DIM), F32)
        for g in range(C_KV_HEADS):
            o = jnp.where(hrow == g, o_all[:, g * HEAD_DIM:(g + 1) * HEAD_DIM], o)
        o_ref[0] = (o * _silu(g_ref[0])).astype(o_ref.dtype)


def _fox_decode(page_table, q3, k_new, v_new, lf_new, g3, cache_k, cache_v, cache_lf):
    nb, n_pages = page_table.shape
    nc = n_pages // DEC_PAGES
    b3 = lambda b, c, pt: (b, 0, 0)
    grid_spec = pltpu.PrefetchScalarGridSpec(
        num_scalar_prefetch=1,
        grid=(nb, nc),
        in_specs=[pl.BlockSpec((1, C_HEADS, HEAD_DIM), b3),
                  pl.BlockSpec((1, 1, C_KV_WIDTH), b3),
                  pl.BlockSpec((1, 1, C_KV_WIDTH), b3),
                  pl.BlockSpec((1, C_HEADS, 1), b3),
                  pl.BlockSpec((1, C_HEADS, HEAD_DIM), b3),
                  pl.BlockSpec(memory_space=pl.ANY),
                  pl.BlockSpec(memory_space=pl.ANY),
                  pl.BlockSpec(memory_space=pl.ANY)],
        out_specs=pl.BlockSpec((1, C_HEADS, HEAD_DIM), b3),
        scratch_shapes=[pltpu.VMEM((2, DEC_PAGES, C_KV_WIDTH, PAGE_SIZE), F32),
                        pltpu.VMEM((2, DEC_PAGES, C_KV_WIDTH, PAGE_SIZE), F32),
                        pltpu.VMEM((2, DEC_PAGES, C_HEADS, PAGE_SIZE), F32),
                        pltpu.SemaphoreType.DMA((3, 2)),
                        pltpu.VMEM((C_HEADS, C_KV_WIDTH), BF16),
                        pltpu.VMEM((C_HEADS, 1), F32),
                        pltpu.VMEM((C_HEADS, 1), F32),
                        pltpu.VMEM((C_HEADS, C_KV_WIDTH), F32),
                        pltpu.VMEM((C_HEADS, 1), F32)])
    return pl.pallas_call(
        _fox_decode_kernel,
        grid_spec=grid_spec,
        out_shape=jax.ShapeDtypeStruct((nb, C_HEADS, HEAD_DIM), BF16),
        compiler_params=_cparams("arbitrary", "arbitrary"),
        name="fox_decode",
    )(page_table, q3, k_new, v_new, lf_new, g3, cache_k, cache_v, cache_lf)


def _split_cols(w, sizes):
    offs = np.cumsum(sizes)[:-1].tolist()
    return jnp.split(w, offs, axis=-1)


def _pad_lanes(a):
    return jnp.pad(a, ((0, 0), (0, LANES - a.shape[-1])))


def _rope_tables(pos):
    half = HEAD_DIM // 2
    inv_freq = ROPE_THETA ** (-jnp.arange(half, dtype=F32) / half)
    ang = pos.astype(F32)[:, None] * inv_freq[None, :]
    cos, sin = jnp.cos(ang), jnp.sin(ang)
    reps = LANES // HEAD_DIM
    return (jnp.tile(jnp.concatenate([cos, cos], axis=1), (1, reps)),
            jnp.tile(jnp.concatenate([-sin, sin], axis=1), (1, reps)))


def _even_weights(w_in, conv_w, conv_b, dt_bias, a_log, ssm_norm_w, w_out, ln_g, ln_b):
    wq, wk, wv, wg, wz, wx, wdt = _split_cols(w_in, EVEN_SPLITS)
    proj = [a.astype(BF16) for a in (wq, wk, wv, wg, wz, wx, _pad_lanes(wdt))]
    wo = w_out.astype(BF16)
    return dict(proj=proj, conv_w=conv_w, conv_b=conv_b[None], dt_bias=_pad_lanes(dt_bias[None]),
                a_log=_pad_lanes(a_log[None]), norm_w=ssm_norm_w[None],
                wo=(wo[:A_WIDTH], wo[A_WIDTH:]), ln_g=ln_g[None], ln_b=ln_b[None])


def _odd_weights(w_in, f_bias, w_out, ln_g, ln_b):
    wq, wk, wv, wf, wg = _split_cols(w_in, ODD_SPLITS)
    proj = [a.astype(BF16) for a in (wq * ATTN_SCALE, wk, wv, _pad_lanes(wf), wg)]
    proj_log2 = [(wq * (ATTN_SCALE * LOG2E)).astype(BF16)] + proj[1:]
    return dict(proj=proj, proj_log2=proj_log2, f_bias=_pad_lanes(f_bias[None]), wo=(w_out.astype(BF16),),
                ln_g=ln_g[None], ln_b=ln_b[None])


def _even_prompt(x, ew, sinks, d_skip):
    b, t, _ = x.shape
    x2 = x.reshape(b * t, D_MODEL)
    cos, sin = _rope_tables(jnp.arange(t, dtype=jnp.int32))
    q, k, v, g, z, xbc, dt = _even_inproj(x2, cos, sin, ew["proj"], tm=512, act_dtype=BF16)
    o_a = _swa_prompt(q, k, v, g, sinks, b, t)
    o_b, state = _ssd_prompt(xbc, dt, z, ew["conv_w"], ew["conv_b"], ew["dt_bias"], ew["a_log"], d_skip,
                             ew["norm_w"], b, t)
    y = _outproj_ln(x2, (o_a, o_b), ew["wo"], ew["ln_g"], ew["ln_b"], tm=512)
    new_k = k.reshape(b, t, A_KV_WIDTH)[:, -WINDOW:].reshape(b, WINDOW, A_KV_HEADS, HEAD_DIM)
    new_v = v.reshape(b, t, A_KV_WIDTH)[:, -WINDOW:].reshape(b, WINDOW, A_KV_HEADS, HEAD_DIM)
    new_conv = xbc.reshape(b, t, CONV_DIM)[:, -(CONV_WIDTH - 1):]
    return y.reshape(b, t, D_MODEL), new_k, new_v, new_conv, state


def _even_sample(x, pos, ew, sinks, d_skip, swa_k, swa_v, conv_hist, ssm_h0):
    b, t, _ = x.shape
    x2 = x.reshape(b, D_MODEL)
    cos, sin = _rope_tables(jnp.full((b,), pos, jnp.int32))
    q, k, v, g, z, xbc, dt = _even_inproj(x2, cos, sin, ew["proj"], tm=b, act_dtype=F32)
    o_a, new_k, new_v = _swa_sample(
        q.reshape(b, A_HEADS, HEAD_DIM), k.reshape(b, 1, A_KV_WIDTH), v.reshape(b, 1, A_KV_WIDTH),
        swa_k.reshape(b, WINDOW, A_KV_WIDTH), swa_v.reshape(b, WINDOW, A_KV_WIDTH),
        g.reshape(b, A_HEADS, HEAD_DIM), sinks, bb=16)
    o_b, new_conv, new_state = _ssd_sample(xbc.reshape(b, 1, CONV_DIM), conv_hist, dt.reshape(b, 1, LANES),
                                           z.reshape(b, 1, B_WIDTH), ssm_h0,
                                           ew["conv_w"], ew["conv_b"], ew["dt_bias"], ew["a_log"], d_skip,
                                           ew["norm_w"], bb=8)
    y = _outproj_ln(x2, (o_a.reshape(b, A_WIDTH), o_b.reshape(b, B_WIDTH)), ew["wo"], ew["ln_g"], ew["ln_b"],
                    tm=b)
    return (y.reshape(b, 1, D_MODEL), new_k.reshape(b, WINDOW, A_KV_HEADS, HEAD_DIM),
            new_v.reshape(b, WINDOW, A_KV_HEADS, HEAD_DIM), new_conv, new_state)


def _odd_prompt(x, ow):
    b, t, _ = x.shape
    x2 = x.reshape(b * t, D_MODEL)
    q_aug, k_aug, vt_aug, k, v, lf, g = _odd_inproj_prompt(x2, ow["proj_log2"], ow["f_bias"], b, t, tm=256)
    o = _fox_prompt(q_aug, k_aug, vt_aug, g, b, t)
    y = _outproj_ln(x2, (o,), ow["wo"], ow["ln_g"], ow["ln_b"], tm=512)
    return (y.reshape(b, t, D_MODEL), k.reshape(b, t, C_KV_HEADS, HEAD_DIM),
            v.reshape(b, t, C_KV_HEADS, HEAD_DIM), lf.reshape(b, t, C_HEADS))


def _odd_sample(x, ow, cache_k, cache_v, cache_lf, page_table):
    b, t, _ = x.shape
    x2 = x.reshape(b, D_MODEL)
    q, k, v, lf, g = _odd_inproj_sample(x2, ow["proj"], ow["f_bias"])
    n_phys = cache_k.shape[0]
    kt = jnp.transpose(cache_k, (0, 2, 3, 1)).reshape(n_phys, C_KV_WIDTH, PAGE_SIZE)
    vt = jnp.transpose(cache_v, (0, 2, 3, 1)).reshape(n_phys, C_KV_WIDTH, PAGE_SIZE)
    lft = jnp.swapaxes(cache_lf, 1, 2)
    o = _fox_decode(page_table, q.reshape(b, C_HEADS, HEAD_DIM), k.reshape(b, 1, C_KV_WIDTH),
                    v.reshape(b, 1, C_KV_WIDTH), lf.reshape(b, C_HEADS, 1), g.reshape(b, C_HEADS, HEAD_DIM),
                    kt, vt, lft)
    y = _outproj_ln(x2, (o.reshape(b, C_WIDTH),), ow["wo"], ow["ln_g"], ow["ln_b"], tm=b)
    return (y.reshape(b, 1, D_MODEL), k.reshape(b, 1, C_KV_HEADS, HEAD_DIM),
            v.reshape(b, 1, C_KV_HEADS, HEAD_DIM), lf.reshape(b, 1, C_HEADS))


def kernel(x_prompt, x_sample, cache_swa_k, cache_swa_v, state_conv, state_ssm, cache_fox_k, cache_fox_v, cache_fox_logf, page_table, w_in_even, attn_sinks, conv_w, conv_b, dt_bias, a_log, d_skip, ssm_norm_w, w_out_even, ln_g_even, ln_b_even, w_in_odd, forget_bias, w_out_odd, ln_g_odd, ln_b_odd):
    past_len = page_table.shape[1] * PAGE_SIZE
    ew = _even_weights(w_in_even[0], conv_w[0], conv_b[0], dt_bias[0], a_log[0], ssm_norm_w[0],
                       w_out_even[0], ln_g_even[0], ln_b_even[0])
    ow = _odd_weights(w_in_odd[0], forget_bias[0], w_out_odd[0], ln_g_odd[0], ln_b_odd[0])

    yp, swa_kp, swa_vp, conv_p, ssm_p = _even_prompt(x_prompt, ew, attn_sinks[0], d_skip[0])
    ys, swa_ks, swa_vs, conv_s, ssm_s = _even_sample(x_sample, past_len, ew, attn_sinks[0], d_skip[0],
                                                     cache_swa_k[0], cache_swa_v[0], state_conv[0], state_ssm[0])
    yp, fox_kp, fox_vp, fox_lp = _odd_prompt(yp, ow)
    ys, fox_ks, fox_vs, fox_ls = _odd_sample(ys, ow, cache_fox_k[0], cache_fox_v[0], cache_fox_logf[0], page_table)

    one = lambda a: a[None]
    return (yp, ys, one(swa_kp), one(swa_vp), one(swa_ks), one(swa_vs), one(conv_p), one(conv_s),
            one(ssm_p), one(ssm_s), one(fox_kp), one(fox_vp), one(fox_lp), one(fox_ks), one(fox_vs), one(fox_ls))
```

```python
import functools

import numpy as np
import jax
import jax.numpy as jnp
from jax import lax
from jax.experimental import pallas as pl
from jax.experimental.pallas import tpu as pltpu

F32 = jnp.float32
BF16 = jnp.bfloat16

D_MODEL = 1024
DEPTH = 2
HEAD_DIM = 64
ATTN_SCALE = HEAD_DIM ** -0.5
A_HEADS, A_KV_HEADS = 8, 2
A_REP = A_HEADS // A_KV_HEADS
A_WIDTH, A_KV_WIDTH = A_HEADS * HEAD_DIM, A_KV_HEADS * HEAD_DIM
WINDOW = 128
ROPE_THETA = 10000.0
B_HEADS, B_HEAD_DIM, B_GROUPS = 8, 64, 2
B_REP = B_HEADS // B_GROUPS
B_WIDTH = B_HEADS * B_HEAD_DIM
D_STATE = 128
CONV_WIDTH = 4
CONV_DIM = B_WIDTH + 2 * B_GROUPS * D_STATE
SSD_CHUNK = 128
C_HEADS, C_KV_HEADS = 16, 4
C_REP = C_HEADS // C_KV_HEADS
C_WIDTH, C_KV_WIDTH = C_HEADS * HEAD_DIM, C_KV_HEADS * HEAD_DIM
PAGE_SIZE = 128
DN_ALPHA = (2 * DEPTH) ** 0.25
LN_EPS = 1e-5
RMS_EPS = 1e-5
EVEN_SPLITS = (A_WIDTH, A_KV_WIDTH, A_KV_WIDTH, A_WIDTH, B_WIDTH, CONV_DIM, B_HEADS)
ODD_SPLITS = (C_WIDTH, C_KV_WIDTH, C_KV_WIDTH, C_HEADS, C_WIDTH)

LANES = 128
VMEM_LIMIT = 56 * 1024 * 1024
C_PIECES = 3
FOX_TQ = 128
LOG2E = 1.4426950408889634
FOX_TK = 512
DEC_PAGES = 32


def _cparams(*sem):
    return pltpu.CompilerParams(dimension_semantics=sem, vmem_limit_bytes=VMEM_LIMIT)


def _const_spec(shape):
    nd = len(shape)
    return pl.BlockSpec(shape, lambda *_: (0,) * nd, pipeline_mode=pl.Buffered(1))


def _silu(x):
    return x * (1.0 / (1.0 + jnp.exp(-x)))


def _softplus(x):
    return jnp.maximum(x, 0.0) + jnp.log1p(jnp.exp(-jnp.abs(x)))


def _log_sigmoid(x):
    return jnp.minimum(x, 0.0) - jnp.log1p(jnp.exp(-jnp.abs(x)))


def _lane_iota(shape):
    return lax.broadcasted_iota(jnp.int32, shape, len(shape) - 1)


def _dot(a, b):
    return jnp.dot(a, b, preferred_element_type=F32)


def _dot_nt(a, b):
    return lax.dot_general(a, b, (((1,), (1,)), ((), ())), preferred_element_type=F32)


def _dot_tn(a, b):
    return lax.dot_general(a, b, (((0,), (0,)), ((), ())), preferred_element_type=F32)


def _split_heads_to_lane_tiles(x, n_pairs):
    out = []
    for j in range(n_pairs):
        col = x[:, j * LANES:(j + 1) * LANES]
        low = _lane_iota(col.shape) < HEAD_DIM
        out.append(jnp.where(low, col, 0.0))
        out.append(jnp.where(low, pltpu.roll(col, HEAD_DIM, axis=1), 0.0))
    return out


def _merge_heads_from_lane_tiles(tiles):
    cols = []
    for j in range(len(tiles) // 2):
        a, b = tiles[2 * j], tiles[2 * j + 1]
        low = _lane_iota(a.shape) < HEAD_DIM
        cols.append(jnp.where(low, a, pltpu.roll(b, HEAD_DIM, axis=1)))
    return jnp.concatenate(cols, axis=1) if len(cols) > 1 else cols[0]


def _rope_cols(x, cos, sin_signed):
    half = HEAD_DIM // 2
    first = (_lane_iota(x.shape) % HEAD_DIM) < half
    swapped = jnp.where(first, pltpu.roll(x, LANES - half, axis=1), pltpu.roll(x, half, axis=1))
    return x * cos + swapped * sin_signed


def _even_inproj_kernel(x_ref, cos_ref, sin_ref, wq, wk, wv, wg, wz, wx, wdt,
                        q_o, k_o, v_o, g_o, z_o, xbc_o, dt_o):
    xb = x_ref[...].astype(BF16)
    cos, sin = cos_ref[...], sin_ref[...]
    q = _dot(xb, wq[...])
    for j in range(A_WIDTH // LANES):
        sl = slice(j * LANES, (j + 1) * LANES)
        q_o[:, sl] = (_rope_cols(q[:, sl], cos, sin) * ATTN_SCALE).astype(q_o.dtype)
    k_o[...] = _rope_cols(_dot(xb, wk[...]), cos, sin)
    v_o[...] = _dot(xb, wv[...])
    g_o[...] = _dot(xb, wg[...]).astype(g_o.dtype)
    z_o[...] = _dot(xb, wz[...]).astype(z_o.dtype)
    xbc_o[...] = _dot(xb, wx[...])
    dt_o[...] = _dot(xb, wdt[...])


def _even_inproj(x2d, cos_tab, sin_tab, w, tm, act_dtype):
    n = x2d.shape[0]
    nblk_seq = cos_tab.shape[0] // tm
    row = lambda i: (i, 0)
    outs = [((n, A_WIDTH), act_dtype), ((n, A_KV_WIDTH), F32), ((n, A_KV_WIDTH), F32),
            ((n, A_WIDTH), act_dtype), ((n, B_WIDTH), act_dtype), ((n, CONV_DIM), F32), ((n, LANES), F32)]
    return pl.pallas_call(
        _even_inproj_kernel,
        grid=(n // tm,),
        in_specs=[pl.BlockSpec((tm, D_MODEL), row),
                  pl.BlockSpec((tm, LANES), lambda i: (i % nblk_seq, 0)),
                  pl.BlockSpec((tm, LANES), lambda i: (i % nblk_seq, 0))]
                 + [_const_spec(a.shape) for a in w],
        out_specs=[pl.BlockSpec((tm, s[1]), row) for s, _ in outs],
        out_shape=[jax.ShapeDtypeStruct(s, d) for s, d in outs],
        compiler_params=_cparams("parallel"),
        name="even_inproj",
    )(x2d, cos_tab, sin_tab, *w)


def _swa_prompt_kernel(sinks_ref, q_ref, kp_ref, kc_ref, vp_ref, vc_ref, g_ref, o_ref):
    n = pl.program_id(1)
    q = q_ref[...]
    kk = jnp.concatenate([kp_ref[...], kc_ref[...]], axis=0).astype(BF16)
    vv = jnp.concatenate([vp_ref[...], vc_ref[...]], axis=0).astype(BF16)
    rows = lax.broadcasted_iota(jnp.int32, (A_REP * WINDOW, 2 * WINDOW), 0)
    kpos = lax.broadcasted_iota(jnp.int32, (A_REP * WINDOW, 2 * WINDOW), 1)
    diff = WINDOW + rows % WINDOW - kpos
    mask = (diff >= 0) & (diff <= WINDOW) & ((kpos >= WINDOW) | (n > 0))
    head_of_row = lax.broadcasted_iota(jnp.int32, (A_REP * WINDOW, 1), 0) // WINDOW
    heads = []
    for kv in range(A_KV_HEADS):
        q_stack = jnp.concatenate([q[:, h * HEAD_DIM:(h + 1) * HEAD_DIM]
                                   for h in range(kv * A_REP, (kv + 1) * A_REP)], axis=0)
        kh = kk[:, kv * HEAD_DIM:(kv + 1) * HEAD_DIM]
        vh = vv[:, kv * HEAD_DIM:(kv + 1) * HEAD_DIM]
        s = jnp.where(mask, _dot_nt(q_stack, kh), -jnp.inf)
        sink = jnp.zeros((A_REP * WINDOW, 1), F32)
        for r in range(A_REP):
            sink = jnp.where(head_of_row == r, sinks_ref[kv * A_REP + r], sink)
        m = jnp.maximum(jnp.max(s, axis=-1, keepdims=True), sink)
        p = jnp.exp(s - m)
        den = jnp.sum(p, axis=-1, keepdims=True) + jnp.exp(sink - m)
        o_stack = _dot((p / den).astype(BF16), vh)
        heads.extend(o_stack[r * WINDOW:(r + 1) * WINDOW] for r in range(A_REP))
    o = jnp.concatenate(heads, axis=1)
    o_ref[...] = (o * _silu(g_ref[...].astype(F32))).astype(o_ref.dtype)


def _swa_prompt(q, k, v, g, sinks, batch, seq):
    nb = seq // WINDOW
    cur = lambda b, n: (b * nb + n, 0)
    prev = lambda b, n: (b * nb + jnp.maximum(n - 1, 0), 0)
    return pl.pallas_call(
        _swa_prompt_kernel,
        grid=(batch, nb),
        in_specs=[pl.BlockSpec(memory_space=pltpu.SMEM),
                  pl.BlockSpec((WINDOW, A_WIDTH), cur),
                  pl.BlockSpec((WINDOW, A_KV_WIDTH), prev),
                  pl.BlockSpec((WINDOW, A_KV_WIDTH), cur),
                  pl.BlockSpec((WINDOW, A_KV_WIDTH), prev),
                  pl.BlockSpec((WINDOW, A_KV_WIDTH), cur),
                  pl.BlockSpec((WINDOW, A_WIDTH), cur)],
        out_specs=pl.BlockSpec((WINDOW, A_WIDTH), cur),
        out_shape=jax.ShapeDtypeStruct((batch * seq, A_WIDTH), BF16),
        compiler_params=_cparams("parallel", "parallel"),
        name="swa_prompt",
    )(sinks, q, k, k, v, v, g)


def _swa_sample_kernel(sinks_ref, q_ref, kn_ref, vn_ref, ck_ref, cv_ref, g_ref, o_ref, nk_ref, nv_ref):
    ck, cv = ck_ref[...], cv_ref[...]
    kn, vn = kn_ref[...], vn_ref[...]
    nk_ref[:, 0:WINDOW - 1, :] = ck[:, 1:WINDOW, :]
    nk_ref[:, WINDOW - 1:WINDOW, :] = kn
    nv_ref[:, 0:WINDOW - 1, :] = cv[:, 1:WINDOW, :]
    nv_ref[:, WINDOW - 1:WINDOW, :] = vn
    q = q_ref[...]
    g = g_ref[...]
    hrow = lax.broadcasted_iota(jnp.int32, (A_REP, 1), 0)
    for kv in range(A_KV_HEADS):
        hs = slice(kv * A_REP, (kv + 1) * A_REP)
        ds = slice(kv * HEAD_DIM, (kv + 1) * HEAD_DIM)
        qh = q[:, hs, :].astype(BF16)
        kh, vh = ck[:, :, ds].astype(BF16), cv[:, :, ds].astype(BF16)
        knh, vnh = kn[:, :, ds].astype(BF16).astype(F32), vn[:, :, ds].astype(BF16).astype(F32)
        s_c = jnp.einsum("bqd,bkd->bqk", qh, kh, preferred_element_type=F32)
        s_n = jnp.sum(qh.astype(F32) * knh, axis=-1, keepdims=True)
        sink = jnp.zeros((A_REP, 1), F32)
        for r in range(A_REP):
            sink = jnp.where(hrow == r, sinks_ref[kv * A_REP + r], sink)
        sink = sink[None]
        m = jnp.maximum(jnp.maximum(jnp.max(s_c, axis=-1, keepdims=True), s_n), sink)
        p_c = jnp.exp(s_c - m)
        p_n = jnp.exp(s_n - m)
        den = jnp.sum(p_c, axis=-1, keepdims=True) + p_n + jnp.exp(sink - m)
        p_c = (p_c / den).astype(BF16)
        p_n = (p_n / den).astype(BF16).astype(F32)
        o = jnp.einsum("bqk,bkd->bqd", p_c, vh, preferred_element_type=F32) + p_n * vnh
        o_ref[:, hs, :] = (o * _silu(g[:, hs, :])).astype(o_ref.dtype)


def _swa_sample(q3, k_new, v_new, cache_k, cache_v, g3, sinks, bb):
    nb = q3.shape[0]
    blk3 = lambda i: (i, 0, 0)
    return pl.pallas_call(
        _swa_sample_kernel,
        grid=(nb // bb,),
        in_specs=[pl.BlockSpec(memory_space=pltpu.SMEM),
                  pl.BlockSpec((bb, A_HEADS, HEAD_DIM), blk3),
                  pl.BlockSpec((bb, 1, A_KV_WIDTH), blk3),
                  pl.BlockSpec((bb, 1, A_KV_WIDTH), blk3),
                  pl.BlockSpec((bb, WINDOW, A_KV_WIDTH), blk3),
                  pl.BlockSpec((bb, WINDOW, A_KV_WIDTH), blk3),
                  pl.BlockSpec((bb, A_HEADS, HEAD_DIM), blk3)],
        out_specs=[pl.BlockSpec((bb, A_HEADS, HEAD_DIM), blk3),
                   pl.BlockSpec((bb, WINDOW, A_KV_WIDTH), blk3),
                   pl.BlockSpec((bb, WINDOW, A_KV_WIDTH), blk3)],
        out_shape=[jax.ShapeDtypeStruct((nb, A_HEADS, HEAD_DIM), F32),
                   jax.ShapeDtypeStruct((nb, WINDOW, A_KV_WIDTH), F32),
                   jax.ShapeDtypeStruct((nb, WINDOW, A_KV_WIDTH), F32)],
        compiler_params=_cparams("parallel"),
        name="swa_sample",
    )(sinks, q3, k_new, v_new, cache_k, cache_v, g3)


def _gated_rmsnorm(y, z, w):
    g = y * _silu(z)
    gw = B_WIDTH // B_GROUPS
    outs = []
    for i in range(B_GROUPS):
        gi = g[..., i * gw:(i + 1) * gw]
        outs.append(gi * lax.rsqrt(jnp.mean(gi * gi, axis=-1, keepdims=True) + RMS_EPS))
    return jnp.concatenate(outs, axis=-1) * w


def _ssd_prompt_kernel(dskip_ref, xbc_ref, dt_ref, z_ref, cw_ref, cb_ref, dtb_ref, alog_ref, nw_ref,
                       o_ref, state_ref, xp_ref):
    L = SSD_CHUNK
    c = pl.program_id(1)

    @pl.when(c == 0)
    def _():
        state_ref[...] = jnp.zeros_like(state_ref)
        xp_ref[0:8, :] = jnp.zeros((8, CONV_DIM), F32)

    x = xbc_ref[...]
    xp_ref[8:8 + L, :] = x
    acc = cb_ref[...] + xp_ref[pl.ds(8 - (CONV_WIDTH - 1), L), :] * cw_ref[0:1, :]
    for j in range(1, CONV_WIDTH):
        acc = acc + xp_ref[pl.ds(8 - (CONV_WIDTH - 1) + j, L), :] * cw_ref[j:j + 1, :]
    xp_ref[0:8, :] = x[L - 8:L, :]
    u = _silu(acc)
    xs = u[:, :B_WIDTH]
    bm = u[:, B_WIDTH:B_WIDTH + B_GROUPS * D_STATE].astype(BF16)
    cm = u[:, B_WIDTH + B_GROUPS * D_STATE:].astype(BF16)

    dt = _softplus(dt_ref[...] + dtb_ref[...])
    da = dt * (-jnp.exp(alog_ref[...]))
    row = lax.broadcasted_iota(jnp.int32, (L, L), 0)
    col = lax.broadcasted_iota(jnp.int32, (L, L), 1)
    causal = row >= col
    cum = jnp.dot(causal.astype(F32), da, precision=lax.Precision.HIGHEST, preferred_element_type=F32)
    cum_t = cum.T
    cb = [_dot_nt(cm[:, g * D_STATE:(g + 1) * D_STATE], bm[:, g * D_STATE:(g + 1) * D_STATE])
          for g in range(B_GROUPS)]

    ys = []
    for h in range(B_HEADS):
        g = h // B_REP
        cum_col = cum[:, h:h + 1]
        cum_row = cum_t[h:h + 1, :]
        cum_end = cum[L - 1:L, h:h + 1]
        xs_h = xs[:, h * B_HEAD_DIM:(h + 1) * B_HEAD_DIM]
        xdt = xs_h * dt[:, h:h + 1]
        decay = jnp.exp(jnp.where(causal, cum_col - cum_row, -jnp.inf))
        y = _dot((cb[g] * decay).astype(BF16), xdt.astype(BF16))
        st = state_ref[0, h]
        cm_g = cm[:, g * D_STATE:(g + 1) * D_STATE]
        y = y + _dot_nt(cm_g, st.astype(BF16)) * jnp.exp(cum_col)
        ys.append(y + dskip_ref[h] * xs_h)
        xw = (xdt * jnp.exp(cum_end - cum_col)).astype(BF16)
        state_ref[0, h] = st * jnp.exp(cum_end) + _dot_tn(xw, bm[:, g * D_STATE:(g + 1) * D_STATE])
    y = jnp.concatenate(ys, axis=1)
    o_ref[...] = _gated_rmsnorm(y, z_ref[...].astype(F32), nw_ref[...]).astype(o_ref.dtype)


def _ssd_prompt(xbc, dt, z, conv_w, conv_b, dt_bias, a_log, d_skip, norm_w, batch, seq):
    nc = seq // SSD_CHUNK
    cur = lambda b, c: (b * nc + c, 0)
    return pl.pallas_call(
        _ssd_prompt_kernel,
        grid=(batch, nc),
        in_specs=[pl.BlockSpec(memory_space=pltpu.SMEM),
                  pl.BlockSpec((SSD_CHUNK, CONV_DIM), cur),
                  pl.BlockSpec((SSD_CHUNK, LANES), cur),
                  pl.BlockSpec((SSD_CHUNK, B_WIDTH), cur),
                  _const_spec(conv_w.shape), _const_spec(conv_b.shape), _const_spec(dt_bias.shape),
                  _const_spec(a_log.shape), _const_spec(norm_w.shape)],
        out_specs=[pl.BlockSpec((SSD_CHUNK, B_WIDTH), cur),
                   pl.BlockSpec((1, B_HEADS, B_HEAD_DIM, D_STATE), lambda b, c: (b, 0, 0, 0))],
        out_shape=[jax.ShapeDtypeStruct((batch * seq, B_WIDTH), BF16),
                   jax.ShapeDtypeStruct((batch, B_HEADS, B_HEAD_DIM, D_STATE), F32)],
        scratch_shapes=[pltpu.VMEM((8 + SSD_CHUNK, CONV_DIM), F32)],
        compiler_params=_cparams("parallel", "arbitrary"),
        name="ssd_prompt",
    )(d_skip, xbc, dt, z, conv_w, conv_b, dt_bias, a_log, norm_w)


def _ssd_sample_kernel(dskip_ref, xbc_ref, hist_ref, dt_ref, z_ref, h0_ref, cw_ref, cb_ref, dtb_ref,
                       alog_ref, nw_ref, o_ref, conv_o, state_o):
    x = xbc_ref[...]
    hist = hist_ref[...]
    acc = cb_ref[...][None] + x * cw_ref[CONV_WIDTH - 1:CONV_WIDTH, :][None]
    for j in range(CONV_WIDTH - 1):
        acc = acc + hist[:, j:j + 1, :] * cw_ref[j:j + 1, :][None]
    conv_o[:, 0:CONV_WIDTH - 2, :] = hist[:, 1:CONV_WIDTH - 1, :]
    conv_o[:, CONV_WIDTH - 2:CONV_WIDTH - 1, :] = x
    u = _silu(acc)
    xs = u[:, :, :B_WIDTH]
    bm = u[:, :, B_WIDTH:B_WIDTH + B_GROUPS * D_STATE]
    cm = u[:, :, B_WIDTH + B_GROUPS * D_STATE:]
    dt = _softplus(dt_ref[...] + dtb_ref[...][None])
    dec = jnp.exp(dt * (-jnp.exp(alog_ref[...]))[None])
    ys = []
    for h in range(B_HEADS):
        g = h // B_REP
        xs_h = xs[:, :, h * B_HEAD_DIM:(h + 1) * B_HEAD_DIM]
        xdt = xs_h * dt[:, :, h:h + 1]
        bm_g = bm[:, :, g * D_STATE:(g + 1) * D_STATE]
        cm_g = cm[:, :, g * D_STATE:(g + 1) * D_STATE]
        dec_h = dec[:, :, h:h + 1]
        st = h0_ref[:, h]
        cb = jnp.sum(cm_g * bm_g, axis=-1, keepdims=True)
        y_off = jnp.einsum("bqn,bpn->bqp", cm_g.astype(BF16), st.astype(BF16),
                           preferred_element_type=F32)
        ys.append(cb * xdt + y_off * dec_h + dskip_ref[h] * xs_h)
        outer = jnp.einsum("bqp,bqn->bpn", xdt.astype(BF16), bm_g.astype(BF16),
                           preferred_element_type=F32)
        state_o[:, h] = st * dec_h + outer
    y = jnp.concatenate(ys, axis=-1)
    o_ref[...] = _gated_rmsnorm(y, z_ref[...], nw_ref[...][None]).astype(o_ref.dtype)


def _ssd_sample(xbc3, hist, dt3, z3, h0, conv_w, conv_b, dt_bias, a_log, d_skip, norm_w, bb):
    nb = xbc3.shape[0]
    b3 = lambda i: (i, 0, 0)
    b4 = lambda i: (i, 0, 0, 0)
    return pl.pallas_call(
        _ssd_sample_kernel,
        grid=(nb // bb,),
        in_specs=[pl.BlockSpec(memory_space=pltpu.SMEM),
                  pl.BlockSpec((bb, 1, CONV_DIM), b3),
                  pl.BlockSpec((bb, CONV_WIDTH - 1, CONV_DIM), b3),
                  pl.BlockSpec((bb, 1, LANES), b3),
                  pl.BlockSpec((bb, 1, B_WIDTH), b3),
                  pl.BlockSpec((bb, B_HEADS, B_HEAD_DIM, D_STATE), b4),
                  _const_spec(conv_w.shape), _const_spec(conv_b.shape), _const_spec(dt_bias.shape),
                  _const_spec(a_log.shape), _const_spec(norm_w.shape)],
        out_specs=[pl.BlockSpec((bb, 1, B_WIDTH), b3),
                   pl.BlockSpec((bb, CONV_WIDTH - 1, CONV_DIM), b3),
                   pl.BlockSpec((bb, B_HEADS, B_HEAD_DIM, D_STATE), b4)],
        out_shape=[jax.ShapeDtypeStruct((nb, 1, B_WIDTH), F32),
                   jax.ShapeDtypeStruct((nb, CONV_WIDTH - 1, CONV_DIM), F32),
                   jax.ShapeDtypeStruct((nb, B_HEADS, B_HEAD_DIM, D_STATE), F32)],
        compiler_params=_cparams("parallel"),
        name="ssd_sample",
    )(d_skip, xbc3, hist, dt3, z3, h0, conv_w, conv_b, dt_bias, a_log, norm_w)


def _outproj_ln_kernel(n_parts, x_ref, *refs):
    parts, ws = refs[:n_parts], refs[n_parts:2 * n_parts]
    g_ref, b_ref, o_ref = refs[2 * n_parts:]
    mix = _dot(parts[0][...].astype(BF16), ws[0][...])
    for a, w in zip(parts[1:], ws[1:]):
        mix = mix + _dot(a[...].astype(BF16), w[...])
    y = DN_ALPHA * x_ref[...] + mix
    mu = jnp.mean(y, axis=-1, keepdims=True)
    var = jnp.mean(jnp.square(y - mu), axis=-1, keepdims=True)
    o_ref[...] = (y - mu) * lax.rsqrt(var + LN_EPS) * g_ref[...] + b_ref[...]


def _outproj_ln(x2d, parts, ws, ln_g, ln_b, tm):
    n = x2d.shape[0]
    row = lambda i: (i, 0)
    return pl.pallas_call(
        functools.partial(_outproj_ln_kernel, len(parts)),
        grid=(n // tm,),
        in_specs=[pl.BlockSpec((tm, D_MODEL), row)]
                 + [pl.BlockSpec((tm, p.shape[1]), row) for p in parts]
                 + [_const_spec(w.shape) for w in ws]
                 + [_const_spec(ln_g.shape), _const_spec(ln_b.shape)],
        out_specs=pl.BlockSpec((tm, D_MODEL), row),
        out_shape=jax.ShapeDtypeStruct((n, D_MODEL), F32),
        compiler_params=_cparams("parallel"),
        name="outproj_ln",
    )(x2d, *parts, *ws, ln_g, ln_b)


def _bf16_pieces(c):
    pieces, rem = [], c
    for _ in range(C_PIECES):
        p = rem.astype(BF16).astype(F32)
        pieces.append(p)
        rem = rem - p
    return pieces


def _odd_inproj_prompt_kernel(tiles_per_seq, x_ref, wq, wk, wv, wf, wg, fb_ref, sq_ref, sk_ref, cq_ref, ck_ref,
                              qa_o, ka_o, vt_o, kt_o, vtf_o, lf_o, g_o, carry_ref):
    i = pl.program_id(0)
    tm = x_ref.shape[0]

    @pl.when(i % tiles_per_seq == 0)
    def _():
        carry_ref[...] = jnp.zeros_like(carry_ref)

    xb = x_ref[...].astype(BF16)
    kf = _dot(xb, wk[...])
    vf = _dot(xb, wv[...])
    g_o[...] = _dot(xb, wg[...]).astype(g_o.dtype)
    lf = _log_sigmoid(_dot(xb, wf[...]) + fb_ref[...])
    lf_o[...] = lf[:, :C_HEADS]

    c = jnp.where(_lane_iota(lf.shape) < C_HEADS, lf, 0.0)
    rows = lax.broadcasted_iota(jnp.int32, c.shape, 0)
    s = 1
    while s < tm:
        c = c + jnp.where(rows >= s, pltpu.roll(c, s, axis=0), 0.0)
        s *= 2
    c = c + carry_ref[...]
    carry_ref[...] = c[tm - 1:tm, :]

    hi, mid, lo = _bf16_pieces(c * LOG2E)
    c3 = (hi + pltpu.roll(mid, C_HEADS, axis=1) + pltpu.roll(lo, 2 * C_HEADS, axis=1)).astype(BF16)
    ex_q = _dot(c3, sq_ref[...]) + cq_ref[...]
    ex_k = _dot(c3, sk_ref[...]) + ck_ref[...]

    qf = _dot(xb, wq[...])
    for h, t in enumerate(_split_heads_to_lane_tiles(qf, C_WIDTH // LANES)):
        sl = slice(h * LANES, (h + 1) * LANES)
        qa_o[:, sl] = (t + ex_q[:, sl]).astype(qa_o.dtype)
    ones_row = (_lane_iota((1, LANES)) == HEAD_DIM).astype(F32)
    k_tiles = _split_heads_to_lane_tiles(kf, C_KV_WIDTH // LANES)
    v_tiles = _split_heads_to_lane_tiles(vf, C_KV_WIDTH // LANES)
    for g in range(C_KV_HEADS):
        sl = slice(g * LANES, (g + 1) * LANES)
        ka_o[:, sl] = (k_tiles[g] + ex_k[:, sl]).astype(ka_o.dtype)
        v_t = (v_tiles[g] + ones_row).T
        vt_o[0, g] = v_t.astype(vt_o.dtype)
        vtf_o[0, g] = v_t[:HEAD_DIM]
        kt_o[0, g] = k_tiles[g].T[:HEAD_DIM]


def _aug_selectors():
    sq = np.zeros((LANES, C_HEADS * LANES), np.float32)
    sk = np.zeros((LANES, C_KV_HEADS * LANES), np.float32)
    cq = np.zeros((1, C_HEADS * LANES), np.float32)
    ck = np.zeros((1, C_KV_HEADS * LANES), np.float32)
    for h in range(C_HEADS):
        g, r = divmod(h, C_REP)
        for p in range(C_PIECES):
            sq[p * C_HEADS + h, h * LANES + HEAD_DIM + p] = 1.0
            col = HEAD_DIM + C_PIECES + C_PIECES * r + p
            cq[0, h * LANES + col] = -1.0
            sk[p * C_HEADS + h, g * LANES + col] = 1.0
    for g in range(C_KV_HEADS):
        for p in range(C_PIECES):
            ck[0, g * LANES + HEAD_DIM + p] = 1.0
    return jnp.asarray(sq, BF16), jnp.asarray(sk, BF16), jnp.asarray(cq), jnp.asarray(ck)


def _odd_inproj_prompt(x2d, w, f_bias, batch, seq, tm):
    n = x2d.shape[0]
    tiles = seq // tm
    sq, sk, cq, ck = _aug_selectors()
    row = lambda i: (i, 0)
    consts = list(w) + [f_bias, sq, sk, cq, ck]
    outs = [((n, C_HEADS * LANES), BF16), ((n, C_KV_HEADS * LANES), BF16),
            ((batch, C_KV_HEADS, LANES, seq), BF16), ((batch, C_KV_HEADS, HEAD_DIM, seq), F32),
            ((batch, C_KV_HEADS, HEAD_DIM, seq), F32), ((n, C_HEADS), F32), ((n, C_WIDTH), BF16)]
    seq_minor = lambda i: (i // tiles, 0, 0, i % tiles)
    out_specs = [pl.BlockSpec((tm, outs[0][0][1]), row), pl.BlockSpec((tm, outs[1][0][1]), row),
                 pl.BlockSpec((1, C_KV_HEADS, LANES, tm), seq_minor),
                 pl.BlockSpec((1, C_KV_HEADS, HEAD_DIM, tm), seq_minor),
                 pl.BlockSpec((1, C_KV_HEADS, HEAD_DIM, tm), seq_minor),
                 pl.BlockSpec((tm, C_HEADS), row), pl.BlockSpec((tm, C_WIDTH), row)]
    return pl.pallas_call(
        functools.partial(_odd_inproj_prompt_kernel, tiles),
        grid=(n // tm,),
        in_specs=[pl.BlockSpec((tm, D_MODEL), row)] + [_const_spec(a.shape) for a in consts],
        out_specs=out_specs,
        out_shape=[jax.ShapeDtypeStruct(s, d) for s, d in outs],
        scratch_shapes=[pltpu.VMEM((1, LANES), F32)],
        compiler_params=_cparams("arbitrary"),
        name="odd_inproj_prompt",
    )(x2d, *consts)


def _odd_inproj_sample_kernel(x_ref, wq, wk, wv, wf, wg, fb_ref, q_o, k_o, v_o, lf_o, g_o):
    xb = x_ref[...].astype(BF16)
    q_o[...] = _dot(xb, wq[...])
    k_o[...] = _dot(xb, wk[...])
    v_o[...] = _dot(xb, wv[...])
    lf_o[...] = _log_sigmoid(_dot(xb, wf[...]) + fb_ref[...])[:, :C_HEADS]
    g_o[...] = _dot(xb, wg[...])


def _odd_inproj_sample(x2d, w, f_bias):
    n = x2d.shape[0]
    outs = [((n, C_WIDTH), F32), ((n, C_KV_WIDTH), F32), ((n, C_KV_WIDTH), F32), ((n, C_HEADS), F32),
            ((n, C_WIDTH), F32)]
    return pl.pallas_call(
        _odd_inproj_sample_kernel,
        out_shape=[jax.ShapeDtypeStruct(s, d) for s, d in outs],
        compiler_params=pltpu.CompilerParams(vmem_limit_bytes=VMEM_LIMIT),
        name="odd_inproj_sample",
    )(x2d, *w, f_bias)


def _fox_prompt_kernel(q_ref, k_ref, vt_ref, g_ref, o_ref, qs_ref, m_ref, alpha_ref, acc_ref, p_ref, s_ref):
    qi = pl.program_id(1)
    tq, tk = FOX_TQ, FOX_TK
    for g in range(C_KV_HEADS):
        for r in range(C_REP):
            h = g * C_REP + r
            qs_ref[g, r * tq:(r + 1) * tq, :] = q_ref[:, h * LANES:(h + 1) * LANES]
    m_ref[...] = jnp.full(m_ref.shape, -jnp.inf, F32)
    acc_ref[...] = jnp.zeros_like(acc_ref)
    n_last = (qi * tq) // tk

    def qk(t, slot, masked):
        start = t * tk if isinstance(t, int) else pl.multiple_of(t * tk, tk)
        for g in range(C_KV_HEADS):
            k_t = k_ref[pl.ds(start, tk), g * LANES:(g + 1) * LANES]
            s_t = _dot_nt(k_t, qs_ref[g])
            if masked:
                krow = lax.broadcasted_iota(jnp.int32, (tk, 1), 0)
                qrel = (qi * tq - start) + _lane_iota((1, C_REP * tq)) % tq
                s_t = jnp.where(krow <= qrel, s_t, -jnp.inf)
            s_ref[slot, g] = s_t

    def softmax(slot):
        for g in range(C_KV_HEADS):
            s_t = s_ref[slot, g]
            m_old = m_ref[g]
            m_new = jnp.maximum(m_old, jnp.max(s_t, axis=0, keepdims=True))
            p_ref[slot, g] = jnp.exp2(s_t - m_new).astype(BF16)
            alpha_ref[slot, g] = jnp.exp2(m_old - m_new)
            m_ref[g] = m_new

    def pv(t, slot):
        start = t * tk if isinstance(t, int) else pl.multiple_of(t * tk, tk)
        for g in range(C_KV_HEADS):
            acc_ref[g] = (acc_ref[g] * alpha_ref[slot, g]
                          + _dot(vt_ref[0, g, :, pl.ds(start, tk)], p_ref[slot, g]))

    def guarded_step(tau, parity):
        @pl.when(tau <= n_last)
        def _():
            qk(tau, parity, True)

        @pl.when((tau >= 1) & (tau <= n_last + 1))
        def _():
            softmax(1 - parity)

        @pl.when((tau >= 2) & (tau <= n_last + 2))
        def _():
            pv(tau - 2, parity)

    def steady_pair(i, carry):
        tau = 2 + 2 * i
        qk(tau, 0, False)
        softmax(1)
        pv(tau - 2, 0)
        qk(tau + 1, 1, False)
        softmax(0)
        pv(tau - 1, 1)
        return carry

    @pl.when(n_last < 2)
    def _():
        for tau in range(4):
            guarded_step(tau, tau % 2)

    @pl.when(n_last >= 2)
    def _():
        qk(0, 0, False)
        qk(1, 1, False)
        softmax(0)
        n_pairs = (n_last - 2) // 2
        lax.fori_loop(0, n_pairs, steady_pair, 0)

        @pl.when(n_last % 2 == 0)
        def _():
            qk(n_last, 0, True)
            softmax(1)
            pv(n_last - 2, 0)
            softmax(0)
            pv(n_last - 1, 1)
            pv(n_last, 0)

        @pl.when(n_last % 2 == 1)
        def _():
            qk(n_last - 1, 0, False)
            softmax(1)
            pv(n_last - 3, 0)
            qk(n_last, 1, True)
            softmax(0)
            pv(n_last - 2, 1)
            softmax(1)
            pv(n_last - 1, 0)
            pv(n_last, 1)

    tiles = []
    for g in range(C_KV_HEADS):
        acc = acc_ref[g]
        for r in range(C_REP):
            t = acc[:, r * tq:(r + 1) * tq].T
            tiles.append(t / t[:, HEAD_DIM:HEAD_DIM + 1])
    o = _merge_heads_from_lane_tiles(tiles)
    o_ref[...] = (o * _silu(g_ref[...].astype(F32))).astype(o_ref.dtype)


def _fox_prompt(q_aug, k_aug, vt_aug, g, batch, seq):
    nq = seq // FOX_TQ
    row = lambda b, i: (b * nq + i, 0)
    once = pl.Buffered(1)
    return pl.pallas_call(
        _fox_prompt_kernel,
        grid=(batch, nq),
        in_specs=[pl.BlockSpec((FOX_TQ, C_HEADS * LANES), row),
                  pl.BlockSpec((seq, C_KV_HEADS * LANES), lambda b, i: (b, 0), pipeline_mode=once),
                  pl.BlockSpec((1, C_KV_HEADS, LANES, seq), lambda b, i: (b, 0, 0, 0), pipeline_mode=once),
                  pl.BlockSpec((FOX_TQ, C_WIDTH), row)],
        out_specs=pl.BlockSpec((FOX_TQ, C_WIDTH), row),
        out_shape=jax.ShapeDtypeStruct((batch * seq, C_WIDTH), BF16),
        scratch_shapes=[pltpu.VMEM((C_KV_HEADS, C_REP * FOX_TQ, LANES), BF16),
                        pltpu.VMEM((C_KV_HEADS, 1, C_REP * FOX_TQ), F32),
                        pltpu.VMEM((2, C_KV_HEADS, 1, C_REP * FOX_TQ), F32),
                        pltpu.VMEM((C_KV_HEADS, LANES, C_REP * FOX_TQ), F32),
                        pltpu.VMEM((2, C_KV_HEADS, FOX_TK, C_REP * FOX_TQ), BF16),
                        pltpu.VMEM((2, C_KV_HEADS, FOX_TK, C_REP * FOX_TQ), F32)],
        compiler_params=_cparams("parallel", "arbitrary"),
        name="fox_prompt",
    )(q_aug, k_aug, vt_aug, g)


def _fox_decode_kernel(pt_ref, q_ref, kn_ref, vn_ref, lfn_ref, g_ref, k_hbm, v_hbm, lf_hbm, o_ref,
                       kbuf, vbuf, lfbuf, sem, qm_ref, m_ref, l_ref, acc_ref, carry_ref):
    b, c = pl.program_id(0), pl.program_id(1)
    nb, nc = pl.num_programs(0), pl.num_programs(1)
    G = DEC_PAGES
    n_pages = nc * G
    step = b * nc + c
    slot = step % 2

    def copies(bb, cc, sl):
        out = []
        for g in range(G):
            page = pt_ref[bb, n_pages - (cc + 1) * G + g]
            out.append((pltpu.make_async_copy(k_hbm.at[page], kbuf.at[sl, g], sem.at[0, sl]), g % 2))
            out.append((pltpu.make_async_copy(v_hbm.at[page], vbuf.at[sl, g], sem.at[1, sl]), (g + 1) % 2))
            out.append((pltpu.make_async_copy(lf_hbm.at[page], lfbuf.at[sl, g], sem.at[2, sl]), 0))
        return out

    @pl.when(step == 0)
    def _():
        for cp, prio in copies(0, 0, 0):
            cp.start(priority=prio)

    @pl.when(step + 1 < nb * nc)
    def _():
        nxt = step + 1
        for cp, prio in copies(nxt // nc, nxt % nc, 1 - slot):
            cp.start(priority=prio)

    @pl.when(c == 0)
    def _():
        q16 = q_ref[0]
        q4 = jnp.concatenate([q16] * C_KV_HEADS, axis=1)
        hrow = lax.broadcasted_iota(jnp.int32, q4.shape, 0) // C_REP
        gcol = _lane_iota(q4.shape) // HEAD_DIM
        qm = jnp.where(hrow == gcol, q4, 0.0).astype(BF16)
        qm_ref[...] = qm
        kn = kn_ref[0].astype(BF16).astype(F32)
        m_ref[...] = jnp.sum(qm.astype(F32) * kn, axis=-1, keepdims=True)
        l_ref[...] = jnp.ones_like(l_ref)
        acc_ref[...] = jnp.broadcast_to(vn_ref[0].astype(BF16).astype(F32), acc_ref.shape)
        carry_ref[...] = lfn_ref[0]

    for cp, _ in copies(b, c, slot):
        cp.wait()

    lf_rows = lfbuf[slot].reshape(G * C_HEADS, PAGE_SIZE)
    hi = lf_rows.astype(BF16)
    lo = (lf_rows - hi.astype(F32)).astype(BF16)
    ii = lax.broadcasted_iota(jnp.int32, (PAGE_SIZE, PAGE_SIZE), 0)
    jj = lax.broadcasted_iota(jnp.int32, (PAGE_SIZE, PAGE_SIZE), 1)
    later = (ii > jj).astype(BF16)
    within = _dot(hi, later) + _dot(lo, later)
    carry = carry_ref[...]
    cols = [None] * G
    for g in reversed(range(G)):
        rs = slice(g * C_HEADS, (g + 1) * C_HEADS)
        cols[g] = within[rs] + carry
        carry = carry + within[rs][:, 0:1] + lf_rows[rs][:, 0:1]
    carry_ref[...] = carry

    kc = jnp.concatenate([kbuf[slot, g].astype(BF16) for g in range(G)], axis=1)
    s = _dot(qm_ref[...], kc) + jnp.concatenate(cols, axis=1)
    m_old = m_ref[...]
    m_new = jnp.maximum(m_old, jnp.max(s, axis=-1, keepdims=True))
    alpha = jnp.exp(m_old - m_new)
    p = jnp.exp(s - m_new)
    l_ref[...] = l_ref[...] * alpha + jnp.sum(p, axis=-1, keepdims=True)
    vc = jnp.concatenate([vbuf[slot, g].astype(BF16) for g in range(G)], axis=1)
    acc_ref[...] = acc_ref[...] * alpha + _dot_nt(p.astype(BF16), vc)
    m_ref[...] = m_new

    @pl.when(c == nc - 1)
    def _():
        o_all = acc_ref[...] / l_ref[...]
        hrow = lax.broadcasted_iota(jnp.int32, (C_HEADS, HEAD_DIM), 0) // C_REP
        o = jnp.zeros((C_HEADS, HEAD_DIM), F32)
        for g in range(C_KV_HEADS):
            o = jnp.where(hrow == g, o_all[:, g * HEAD_DIM:(g + 1) * HEAD_DIM], o)
        o_ref[0] = (o * _silu(g_ref[0])).astype(o_ref.dtype)


def _fox_decode(page_table, q3, k_new, v_new, lf_new, g3, cache_k, cache_v, cache_lf):
    nb, n_pages = page_table.shape
    nc = n_pages // DEC_PAGES
    b3 = lambda b, c, pt: (b, 0, 0)
    grid_spec = pltpu.PrefetchScalarGridSpec(
        num_scalar_prefetch=1,
        grid=(nb, nc),
        in_specs=[pl.BlockSpec((1, C_HEADS, HEAD_DIM), b3),
                  pl.BlockSpec((1, 1, C_KV_WIDTH), b3),
                  pl.BlockSpec((1, 1, C_KV_WIDTH), b3),
                  pl.BlockSpec((1, C_HEADS, 1), b3),
                  pl.BlockSpec((1, C_HEADS, HEAD_DIM), b3),
                  pl.BlockSpec(memory_space=pl.ANY),
                  pl.BlockSpec(memory_space=pl.ANY),
                  pl.BlockSpec(memory_space=pl.ANY)],
        out_specs=pl.BlockSpec((1, C_HEADS, HEAD_DIM), b3),
        scratch_shapes=[pltpu.VMEM((2, DEC_PAGES, C_KV_WIDTH, PAGE_SIZE), F32),
                        pltpu.VMEM((2, DEC_PAGES, C_KV_WIDTH, PAGE_SIZE), F32),
                        pltpu.VMEM((2, DEC_PAGES, C_HEADS, PAGE_SIZE), F32),
                        pltpu.SemaphoreType.DMA((3, 2)),
                        pltpu.VMEM((C_HEADS, C_KV_WIDTH), BF16),
                        pltpu.VMEM((C_HEADS, 1), F32),
                        pltpu.VMEM((C_HEADS, 1), F32),
                        pltpu.VMEM((C_HEADS, C_KV_WIDTH), F32),
                        pltpu.VMEM((C_HEADS, 1), F32)])
    return pl.pallas_call(
        _fox_decode_kernel,
        grid_spec=grid_spec,
        out_shape=jax.ShapeDtypeStruct((nb, C_HEADS, HEAD_DIM), BF16),
        compiler_params=_cparams("arbitrary", "arbitrary"),
        name="fox_decode",
    )(page_table, q3, k_new, v_new, lf_new, g3, cache_k, cache_v, cache_lf)


def _split_cols(w, sizes):
    offs = np.cumsum(sizes)[:-1].tolist()
    return jnp.split(w, offs, axis=-1)


def _pad_lanes(a):
    return jnp.pad(a, ((0, 0), (0, LANES - a.shape[-1])))


def _rope_tables(pos):
    half = HEAD_DIM // 2
    inv_freq = ROPE_THETA ** (-jnp.arange(half, dtype=F32) / half)
    ang = pos.astype(F32)[:, None] * inv_freq[None, :]
    cos, sin = jnp.cos(ang), jnp.sin(ang)
    reps = LANES // HEAD_DIM
    return (jnp.tile(jnp.concatenate([cos, cos], axis=1), (1, reps)),
            jnp.tile(jnp.concatenate([-sin, sin], axis=1), (1, reps)))


def _even_weights(w_in, conv_w, conv_b, dt_bias, a_log, ssm_norm_w, w_out, ln_g, ln_b):
    wq, wk, wv, wg, wz, wx, wdt = _split_cols(w_in, EVEN_SPLITS)
    proj = [a.astype(BF16) for a in (wq, wk, wv, wg, wz, wx, _pad_lanes(wdt))]
    wo = w_out.astype(BF16)
    return dict(proj=proj, conv_w=conv_w, conv_b=conv_b[None], dt_bias=_pad_lanes(dt_bias[None]),
                a_log=_pad_lanes(a_log[None]), norm_w=ssm_norm_w[None],
                wo=(wo[:A_WIDTH], wo[A_WIDTH:]), ln_g=ln_g[None], ln_b=ln_b[None])


def _odd_weights(w_in, f_bias, w_out, ln_g, ln_b):
    wq, wk, wv, wf, wg = _split_cols(w_in, ODD_SPLITS)
    proj = [a.astype(BF16) for a in (wq * ATTN_SCALE, wk, wv, _pad_lanes(wf), wg)]
    proj_log2 = [(wq * (ATTN_SCALE * LOG2E)).astype(BF16)] + proj[1:]
    return dict(proj=proj, proj_log2=proj_log2, f_bias=_pad_lanes(f_bias[None]), wo=(w_out.astype(BF16),),
                ln_g=ln_g[None], ln_b=ln_b[None])


def _even_prompt(x, ew, sinks, d_skip):
    b, t, _ = x.shape
    x2 = x.reshape(b * t, D_MODEL)
    cos, sin = _rope_tables(jnp.arange(t, dtype=jnp.int32))
    q, k, v, g, z, xbc, dt = _even_inproj(x2, cos, sin, ew["proj"], tm=1024, act_dtype=BF16)
    o_a = _swa_prompt(q, k, v, g, sinks, b, t)
    o_b, state = _ssd_prompt(xbc, dt, z, ew["conv_w"], ew["conv_b"], ew["dt_bias"], ew["a_log"], d_skip,
                             ew["norm_w"], b, t)
    y = _outproj_ln(x2, (o_a, o_b), ew["wo"], ew["ln_g"], ew["ln_b"], tm=512)
    new_k = k.reshape(b, t, A_KV_WIDTH)[:, -WINDOW:].reshape(b, WINDOW, A_KV_HEADS, HEAD_DIM)
    new_v = v.reshape(b, t, A_KV_WIDTH)[:, -WINDOW:].reshape(b, WINDOW, A_KV_HEADS, HEAD_DIM)
    new_conv = xbc.reshape(b, t, CONV_DIM)[:, -(CONV_WIDTH - 1):]
    return y.reshape(b, t, D_MODEL), new_k, new_v, new_conv, state


def _even_sample(x, pos, ew, sinks, d_skip, swa_k, swa_v, conv_hist, ssm_h0):
    b, t, _ = x.shape
    x2 = x.reshape(b, D_MODEL)
    cos, sin = _rope_tables(jnp.full((b,), pos, jnp.int32))
    q, k, v, g, z, xbc, dt = _even_inproj(x2, cos, sin, ew["proj"], tm=b, act_dtype=F32)
    o_a, new_k, new_v = _swa_sample(
        q.reshape(b, A_HEADS, HEAD_DIM), k.reshape(b, 1, A_KV_WIDTH), v.reshape(b, 1, A_KV_WIDTH),
        swa_k.reshape(b, WINDOW, A_KV_WIDTH), swa_v.reshape(b, WINDOW, A_KV_WIDTH),
        g.reshape(b, A_HEADS, HEAD_DIM), sinks, bb=16)
    o_b, new_conv, new_state = _ssd_sample(xbc.reshape(b, 1, CONV_DIM), conv_hist, dt.reshape(b, 1, LANES),
                                           z.reshape(b, 1, B_WIDTH), ssm_h0,
                                           ew["conv_w"], ew["conv_b"], ew["dt_bias"], ew["a_log"], d_skip,
                                           ew["norm_w"], bb=8)
    y = _outproj_ln(x2, (o_a.reshape(b, A_WIDTH), o_b.reshape(b, B_WIDTH)), ew["wo"], ew["ln_g"], ew["ln_b"],
                    tm=b)
    return (y.reshape(b, 1, D_MODEL), new_k.reshape(b, WINDOW, A_KV_HEADS, HEAD_DIM),
            new_v.reshape(b, WINDOW, A_KV_HEADS, HEAD_DIM), new_conv, new_state)


def _odd_prompt(x, ow):
    b, t, _ = x.shape
    x2 = x.reshape(b * t, D_MODEL)
    q_aug, k_aug, vt_aug, kt, vt, lf, g = _odd_inproj_prompt(x2, ow["proj_log2"], ow["f_bias"], b, t, tm=512)
    o = _fox_prompt(q_aug, k_aug, vt_aug, g, b, t)
    y = _outproj_ln(x2, (o,), ow["wo"], ow["ln_g"], ow["ln_b"], tm=512)
    return (y.reshape(b, t, D_MODEL), jnp.transpose(kt, (0, 3, 1, 2)), jnp.transpose(vt, (0, 3, 1, 2)),
            lf.reshape(b, t, C_HEADS))


def _odd_sample(x, ow, cache_k, cache_v, cache_lf, page_table):
    b, t, _ = x.shape
    x2 = x.reshape(b, D_MODEL)
    q, k, v, lf, g = _odd_inproj_sample(x2, ow["proj"], ow["f_bias"])
    n_phys = cache_k.shape[0]
    kt = jnp.transpose(cache_k, (0, 2, 3, 1)).reshape(n_phys, C_KV_WIDTH, PAGE_SIZE)
    vt = jnp.transpose(cache_v, (0, 2, 3, 1)).reshape(n_phys, C_KV_WIDTH, PAGE_SIZE)
    lft = jnp.swapaxes(cache_lf, 1, 2)
    o = _fox_decode(page_table, q.reshape(b, C_HEADS, HEAD_DIM), k.reshape(b, 1, C_KV_WIDTH),
                    v.reshape(b, 1, C_KV_WIDTH), lf.reshape(b, C_HEADS, 1), g.reshape(b, C_HEADS, HEAD_DIM),
                    kt, vt, lft)
    y = _outproj_ln(x2, (o.reshape(b, C_WIDTH),), ow["wo"], ow["ln_g"], ow["ln_b"], tm=b)
    return (y.reshape(b, 1, D_MODEL), k.reshape(b, 1, C_KV_HEADS, HEAD_DIM),
            v.reshape(b, 1, C_KV_HEADS, HEAD_DIM), lf.reshape(b, 1, C_HEADS))


def kernel(x_prompt, x_sample, cache_swa_k, cache_swa_v, state_conv, state_ssm, cache_fox_k, cache_fox_v, cache_fox_logf, page_table, w_in_even, attn_sinks, conv_w, conv_b, dt_bias, a_log, d_skip, ssm_norm_w, w_out_even, ln_g_even, ln_b_even, w_in_odd, forget_bias, w_out_odd, ln_g_odd, ln_b_odd):
    past_len = page_table.shape[1] * PAGE_SIZE
    ew = _even_weights(w_in_even[0], conv_w[0], conv_b[0], dt_bias[0], a_log[0], ssm_norm_w[0],
                       w_out_even[0], ln_g_even[0], ln_b_even[0])
    ow = _odd_weights(w_in_odd[0], forget_bias[0], w_out_odd[0], ln_g_odd[0], ln_b_odd[0])

    yp, swa_kp, swa_vp, conv_p, ssm_p = _even_prompt(x_prompt, ew, attn_sinks[0], d_skip[0])
    ys, swa_ks, swa_vs, conv_s, ssm_s = _even_sample(x_sample, past_len, ew, attn_sinks[0], d_skip[0],
                                                     cache_swa_k[0], cache_swa_v[0], state_conv[0], state_ssm[0])
    yp, fox_kp, fox_vp, fox_lp = _odd_prompt(yp, ow)
    ys, fox_ks, fox_vs, fox_ls = _odd_sample(ys, ow, cache_fox_k[0], cache_fox_v[0], cache_fox_logf[0], page_table)

    one = lambda a: a[None]
    return (yp, ys, one(swa_kp), one(swa_vp), one(swa_ks), one(swa_vs), one(conv_p), one(conv_s),
            one(ssm_p), one(ssm_s), one(fox_kp), one(fox_vp), one(fox_lp), one(fox_ks), one(fox_vs), one(fox_ls))
```

```python
import functools

import numpy as np
import jax
import jax.numpy as jnp
from jax import lax
from jax.experimental import pallas as pl
from jax.experimental.pallas import tpu as pltpu

F32 = jnp.float32
BF16 = jnp.bfloat16

D_MODEL = 1024
DEPTH = 2
HEAD_DIM = 64
ATTN_SCALE = HEAD_DIM ** -0.5
A_HEADS, A_KV_HEADS = 8, 2
A_REP = A_HEADS // A_KV_HEADS
A_WIDTH, A_KV_WIDTH = A_HEADS * HEAD_DIM, A_KV_HEADS * HEAD_DIM
WINDOW = 128
ROPE_THETA = 10000.0
B_HEADS, B_HEAD_DIM, B_GROUPS = 8, 64, 2
B_REP = B_HEADS // B_GROUPS
B_WIDTH = B_HEADS * B_HEAD_DIM
D_STATE = 128
CONV_WIDTH = 4
CONV_DIM = B_WIDTH + 2 * B_GROUPS * D_STATE
SSD_CHUNK = 128
C_HEADS, C_KV_HEADS = 16, 4
C_REP = C_HEADS // C_KV_HEADS
C_WIDTH, C_KV_WIDTH = C_HEADS * HEAD_DIM, C_KV_HEADS * HEAD_DIM
PAGE_SIZE = 128
DN_ALPHA = (2 * DEPTH) ** 0.25
LN_EPS = 1e-5
RMS_EPS = 1e-5
EVEN_SPLITS = (A_WIDTH, A_KV_WIDTH, A_KV_WIDTH, A_WIDTH, B_WIDTH, CONV_DIM, B_HEADS)
ODD_SPLITS = (C_WIDTH, C_KV_WIDTH, C_KV_WIDTH, C_HEADS, C_WIDTH)

LANES = 128
VMEM_LIMIT = 56 * 1024 * 1024
C_PIECES = 3
FOX_TQ = 128
LOG2E = 1.4426950408889634
FOX_TK = 512
DEC_PAGES = 32


def _cparams(*sem):
    return pltpu.CompilerParams(dimension_semantics=sem, vmem_limit_bytes=VMEM_LIMIT)


def _const_spec(shape):
    nd = len(shape)
    return pl.BlockSpec(shape, lambda *_: (0,) * nd, pipeline_mode=pl.Buffered(1))


def _silu(x):
    return x * (1.0 / (1.0 + jnp.exp(-x)))


def _softplus(x):
    return jnp.maximum(x, 0.0) + jnp.log1p(jnp.exp(-jnp.abs(x)))


def _log_sigmoid(x):
    return jnp.minimum(x, 0.0) - jnp.log1p(jnp.exp(-jnp.abs(x)))


def _lane_iota(shape):
    return lax.broadcasted_iota(jnp.int32, shape, len(shape) - 1)


def _dot(a, b):
    return jnp.dot(a, b, preferred_element_type=F32)


def _dot_nt(a, b):
    return lax.dot_general(a, b, (((1,), (1,)), ((), ())), preferred_element_type=F32)


def _dot_tn(a, b):
    return lax.dot_general(a, b, (((0,), (0,)), ((), ())), preferred_element_type=F32)


def _split_heads_to_lane_tiles(x, n_pairs):
    out = []
    for j in range(n_pairs):
        col = x[:, j * LANES:(j + 1) * LANES]
        low = _lane_iota(col.shape) < HEAD_DIM
        out.append(jnp.where(low, col, 0.0))
        out.append(jnp.where(low, pltpu.roll(col, HEAD_DIM, axis=1), 0.0))
    return out


def _merge_heads_from_lane_tiles(tiles):
    cols = []
    for j in range(len(tiles) // 2):
        a, b = tiles[2 * j], tiles[2 * j + 1]
        low = _lane_iota(a.shape) < HEAD_DIM
        cols.append(jnp.where(low, a, pltpu.roll(b, HEAD_DIM, axis=1)))
    return jnp.concatenate(cols, axis=1) if len(cols) > 1 else cols[0]


def _rope_cols(x, cos, sin_signed):
    half = HEAD_DIM // 2
    first = (_lane_iota(x.shape) % HEAD_DIM) < half
    swapped = jnp.where(first, pltpu.roll(x, LANES - half, axis=1), pltpu.roll(x, half, axis=1))
    return x * cos + swapped * sin_signed


def _even_inproj_kernel(x_ref, cos_ref, sin_ref, wq, wk, wv, wg, wz, wx, wdt,
                        q_o, k_o, v_o, g_o, z_o, xbc_o, dt_o):
    xb = x_ref[...].astype(BF16)
    cos, sin = cos_ref[...], sin_ref[...]
    q = _dot(xb, wq[...])
    for j in range(A_WIDTH // LANES):
        sl = slice(j * LANES, (j + 1) * LANES)
        q_o[:, sl] = (_rope_cols(q[:, sl], cos, sin) * ATTN_SCALE).astype(q_o.dtype)
    k_o[...] = _rope_cols(_dot(xb, wk[...]), cos, sin)
    v_o[...] = _dot(xb, wv[...])
    g_o[...] = _dot(xb, wg[...]).astype(g_o.dtype)
    z_o[...] = _dot(xb, wz[...]).astype(z_o.dtype)
    xbc_o[...] = _dot(xb, wx[...])
    dt_o[...] = _dot(xb, wdt[...])


def _even_inproj(x2d, cos_tab, sin_tab, w, tm, act_dtype):
    n = x2d.shape[0]
    nblk_seq = cos_tab.shape[0] // tm
    row = lambda i: (i, 0)
    outs = [((n, A_WIDTH), act_dtype), ((n, A_KV_WIDTH), F32), ((n, A_KV_WIDTH), F32),
            ((n, A_WIDTH), act_dtype), ((n, B_WIDTH), act_dtype), ((n, CONV_DIM), F32), ((n, LANES), F32)]
    return pl.pallas_call(
        _even_inproj_kernel,
        grid=(n // tm,),
        in_specs=[pl.BlockSpec((tm, D_MODEL), row),
                  pl.BlockSpec((tm, LANES), lambda i: (i % nblk_seq, 0)),
                  pl.BlockSpec((tm, LANES), lambda i: (i % nblk_seq, 0))]
                 + [_const_spec(a.shape) for a in w],
        out_specs=[pl.BlockSpec((tm, s[1]), row) for s, _ in outs],
        out_shape=[jax.ShapeDtypeStruct(s, d) for s, d in outs],
        compiler_params=_cparams("parallel"),
        name="even_inproj",
    )(x2d, cos_tab, sin_tab, *w)


def _swa_prompt_kernel(sinks_ref, q_ref, kp_ref, kc_ref, vp_ref, vc_ref, g_ref, o_ref):
    n = pl.program_id(1)
    q = q_ref[...]
    kk = jnp.concatenate([kp_ref[...], kc_ref[...]], axis=0).astype(BF16)
    vv = jnp.concatenate([vp_ref[...], vc_ref[...]], axis=0).astype(BF16)
    qpos = WINDOW + lax.broadcasted_iota(jnp.int32, (WINDOW, 2 * WINDOW), 0)
    kpos = lax.broadcasted_iota(jnp.int32, (WINDOW, 2 * WINDOW), 1)
    diff = qpos - kpos
    band = (diff >= 0) & (diff <= WINDOW) & ((kpos >= WINDOW) | (n > 0))
    bias = jnp.concatenate([jnp.where(band, 0.0, -jnp.inf)] * A_REP, axis=0)
    head_of_row = lax.broadcasted_iota(jnp.int32, (A_REP * WINDOW, 1), 0) // WINDOW
    heads = []
    for kv in range(A_KV_HEADS):
        q_stack = jnp.concatenate([q[:, h * HEAD_DIM:(h + 1) * HEAD_DIM]
                                   for h in range(kv * A_REP, (kv + 1) * A_REP)], axis=0)
        kh = kk[:, kv * HEAD_DIM:(kv + 1) * HEAD_DIM]
        vh = vv[:, kv * HEAD_DIM:(kv + 1) * HEAD_DIM]
        s = _dot_nt(q_stack, kh) + bias
        sink = jnp.zeros((A_REP * WINDOW, 1), F32)
        for r in range(A_REP):
            sink = jnp.where(head_of_row == r, sinks_ref[kv * A_REP + r], sink)
        m = jnp.maximum(jnp.max(s, axis=-1, keepdims=True), sink)
        p = jnp.exp(s - m)
        den = jnp.sum(p, axis=-1, keepdims=True) + jnp.exp(sink - m)
        o_stack = _dot(p.astype(BF16), vh) / den
        heads.extend(o_stack[r * WINDOW:(r + 1) * WINDOW] for r in range(A_REP))
    o = jnp.concatenate(heads, axis=1)
    o_ref[...] = (o * _silu(g_ref[...].astype(F32))).astype(o_ref.dtype)


def _swa_prompt(q, k, v, g, sinks, batch, seq):
    nb = seq // WINDOW
    cur = lambda b, n: (b * nb + n, 0)
    prev = lambda b, n: (b * nb + jnp.maximum(n - 1, 0), 0)
    return pl.pallas_call(
        _swa_prompt_kernel,
        grid=(batch, nb),
        in_specs=[pl.BlockSpec(memory_space=pltpu.SMEM),
                  pl.BlockSpec((WINDOW, A_WIDTH), cur),
                  pl.BlockSpec((WINDOW, A_KV_WIDTH), prev),
                  pl.BlockSpec((WINDOW, A_KV_WIDTH), cur),
                  pl.BlockSpec((WINDOW, A_KV_WIDTH), prev),
                  pl.BlockSpec((WINDOW, A_KV_WIDTH), cur),
                  pl.BlockSpec((WINDOW, A_WIDTH), cur)],
        out_specs=pl.BlockSpec((WINDOW, A_WIDTH), cur),
        out_shape=jax.ShapeDtypeStruct((batch * seq, A_WIDTH), BF16),
        compiler_params=_cparams("parallel", "parallel"),
        name="swa_prompt",
    )(sinks, q, k, k, v, v, g)


def _swa_sample_kernel(sinks_ref, q_ref, kn_ref, vn_ref, ck_ref, cv_ref, g_ref, o_ref, nk_ref, nv_ref):
    ck, cv = ck_ref[...], cv_ref[...]
    kn, vn = kn_ref[...], vn_ref[...]
    nk_ref[:, 0:WINDOW - 1, :] = ck[:, 1:WINDOW, :]
    nk_ref[:, WINDOW - 1:WINDOW, :] = kn
    nv_ref[:, 0:WINDOW - 1, :] = cv[:, 1:WINDOW, :]
    nv_ref[:, WINDOW - 1:WINDOW, :] = vn
    q = q_ref[...]
    g = g_ref[...]
    hrow = lax.broadcasted_iota(jnp.int32, (A_REP, 1), 0)
    for kv in range(A_KV_HEADS):
        hs = slice(kv * A_REP, (kv + 1) * A_REP)
        ds = slice(kv * HEAD_DIM, (kv + 1) * HEAD_DIM)
        qh = q[:, hs, :].astype(BF16)
        kh, vh = ck[:, :, ds].astype(BF16), cv[:, :, ds].astype(BF16)
        knh, vnh = kn[:, :, ds].astype(BF16).astype(F32), vn[:, :, ds].astype(BF16).astype(F32)
        s_c = jnp.einsum("bqd,bkd->bqk", qh, kh, preferred_element_type=F32)
        s_n = jnp.sum(qh.astype(F32) * knh, axis=-1, keepdims=True)
        sink = jnp.zeros((A_REP, 1), F32)
        for r in range(A_REP):
            sink = jnp.where(hrow == r, sinks_ref[kv * A_REP + r], sink)
        sink = sink[None]
        m = jnp.maximum(jnp.maximum(jnp.max(s_c, axis=-1, keepdims=True), s_n), sink)
        p_c = jnp.exp(s_c - m)
        p_n = jnp.exp(s_n - m)
        den = jnp.sum(p_c, axis=-1, keepdims=True) + p_n + jnp.exp(sink - m)
        p_c = (p_c / den).astype(BF16)
        p_n = (p_n / den).astype(BF16).astype(F32)
        o = jnp.einsum("bqk,bkd->bqd", p_c, vh, preferred_element_type=F32) + p_n * vnh
        o_ref[:, hs, :] = (o * _silu(g[:, hs, :])).astype(o_ref.dtype)


def _swa_sample(q3, k_new, v_new, cache_k, cache_v, g3, sinks, bb):
    nb = q3.shape[0]
    blk3 = lambda i: (i, 0, 0)
    return pl.pallas_call(
        _swa_sample_kernel,
        grid=(nb // bb,),
        in_specs=[pl.BlockSpec(memory_space=pltpu.SMEM),
                  pl.BlockSpec((bb, A_HEADS, HEAD_DIM), blk3),
                  pl.BlockSpec((bb, 1, A_KV_WIDTH), blk3),
                  pl.BlockSpec((bb, 1, A_KV_WIDTH), blk3),
                  pl.BlockSpec((bb, WINDOW, A_KV_WIDTH), blk3),
                  pl.BlockSpec((bb, WINDOW, A_KV_WIDTH), blk3),
                  pl.BlockSpec((bb, A_HEADS, HEAD_DIM), blk3)],
        out_specs=[pl.BlockSpec((bb, A_HEADS, HEAD_DIM), blk3),
                   pl.BlockSpec((bb, WINDOW, A_KV_WIDTH), blk3),
                   pl.BlockSpec((bb, WINDOW, A_KV_WIDTH), blk3)],
        out_shape=[jax.ShapeDtypeStruct((nb, A_HEADS, HEAD_DIM), F32),
                   jax.ShapeDtypeStruct((nb, WINDOW, A_KV_WIDTH), F32),
                   jax.ShapeDtypeStruct((nb, WINDOW, A_KV_WIDTH), F32)],
        compiler_params=_cparams("parallel"),
        name="swa_sample",
    )(sinks, q3, k_new, v_new, cache_k, cache_v, g3)


def _gated_rmsnorm(y, z, w):
    g = y * _silu(z)
    gw = B_WIDTH // B_GROUPS
    outs = []
    for i in range(B_GROUPS):
        gi = g[..., i * gw:(i + 1) * gw]
        outs.append(gi * lax.rsqrt(jnp.mean(gi * gi, axis=-1, keepdims=True) + RMS_EPS))
    return jnp.concatenate(outs, axis=-1) * w


def _ssd_prompt_kernel(dskip_ref, xbc_ref, dt_ref, z_ref, cw_ref, cb_ref, dtb_ref, alog_ref, nw_ref,
                       o_ref, state_ref, xp_ref):
    L = SSD_CHUNK
    c = pl.program_id(1)

    @pl.when(c == 0)
    def _():
        state_ref[...] = jnp.zeros_like(state_ref)
        xp_ref[0:8, :] = jnp.zeros((8, CONV_DIM), F32)

    x = xbc_ref[...]
    xp_ref[8:8 + L, :] = x
    acc = cb_ref[...] + xp_ref[pl.ds(8 - (CONV_WIDTH - 1), L), :] * cw_ref[0:1, :]
    for j in range(1, CONV_WIDTH):
        acc = acc + xp_ref[pl.ds(8 - (CONV_WIDTH - 1) + j, L), :] * cw_ref[j:j + 1, :]
    xp_ref[0:8, :] = x[L - 8:L, :]
    u = _silu(acc)
    xs = u[:, :B_WIDTH]
    bm = u[:, B_WIDTH:B_WIDTH + B_GROUPS * D_STATE].astype(BF16)
    cm = u[:, B_WIDTH + B_GROUPS * D_STATE:].astype(BF16)

    dt = _softplus(dt_ref[...] + dtb_ref[...])
    da = dt * (-jnp.exp(alog_ref[...]))
    row = lax.broadcasted_iota(jnp.int32, (L, L), 0)
    col = lax.broadcasted_iota(jnp.int32, (L, L), 1)
    causal = row >= col
    cum = jnp.dot(causal.astype(F32), da, precision=lax.Precision.HIGHEST, preferred_element_type=F32)
    cum_t = cum.T
    cb = [_dot_nt(cm[:, g * D_STATE:(g + 1) * D_STATE], bm[:, g * D_STATE:(g + 1) * D_STATE])
          for g in range(B_GROUPS)]

    P = B_HEAD_DIM
    low = _lane_iota((L, 2 * P)) < P
    top = lax.broadcasted_iota(jnp.int32, (2 * P, 1), 0) < P
    ys = []
    for j in range(B_HEADS // 2):
        h0, h1 = 2 * j, 2 * j + 1
        g = h0 // B_REP
        cm_g = cm[:, g * D_STATE:(g + 1) * D_STATE]
        bm_g = bm[:, g * D_STATE:(g + 1) * D_STATE]
        cumb = [jnp.broadcast_to(cum[:, h:h + 1], (L, 2 * P)) for h in (h0, h1)]
        cum_end = [cum[L - 1:L, h:h + 1] for h in (h0, h1)]
        xs_pair = xs[:, j * 2 * P:(j + 1) * 2 * P]
        xdt = xs_pair * jnp.where(low, dt[:, h0:h0 + 1], dt[:, h1:h1 + 1])
        y = _dot_nt(cm_g, jnp.concatenate([state_ref[0, h0], state_ref[0, h1]], axis=0).astype(BF16))
        y = y * jnp.exp(jnp.where(low, cumb[0], cumb[1]))
        for e, h in enumerate((h0, h1)):
            decay = jnp.exp(jnp.where(causal, cumb[e] - cum_t[h:h + 1, :], -jnp.inf))
            x_h = jnp.where(low if e == 0 else ~low, xdt, 0.0)
            y = y + _dot((cb[g] * decay).astype(BF16), x_h.astype(BF16))
        d_pair = jnp.where(low[0:1], dskip_ref[h0], dskip_ref[h1])
        ys.append(y + d_pair * xs_pair)
        xw = xdt * jnp.exp(jnp.where(low, cum_end[0] - cumb[0], cum_end[1] - cumb[1]))
        new = _dot_tn(xw.astype(BF16), bm_g)
        keep = jnp.where(top, jnp.exp(cum_end[0]), jnp.exp(cum_end[1]))
        state_ref[0, h0] = state_ref[0, h0] * keep[:P] + new[:P]
        state_ref[0, h1] = state_ref[0, h1] * keep[P:] + new[P:]
    y = jnp.concatenate(ys, axis=1)
    o_ref[...] = _gated_rmsnorm(y, z_ref[...].astype(F32), nw_ref[...]).astype(o_ref.dtype)


def _ssd_prompt(xbc, dt, z, conv_w, conv_b, dt_bias, a_log, d_skip, norm_w, batch, seq):
    nc = seq // SSD_CHUNK
    cur = lambda b, c: (b * nc + c, 0)
    return pl.pallas_call(
        _ssd_prompt_kernel,
        grid=(batch, nc),
        in_specs=[pl.BlockSpec(memory_space=pltpu.SMEM),
                  pl.BlockSpec((SSD_CHUNK, CONV_DIM), cur),
                  pl.BlockSpec((SSD_CHUNK, LANES), cur),
                  pl.BlockSpec((SSD_CHUNK, B_WIDTH), cur),
                  _const_spec(conv_w.shape), _const_spec(conv_b.shape), _const_spec(dt_bias.shape),
                  _const_spec(a_log.shape), _const_spec(norm_w.shape)],
        out_specs=[pl.BlockSpec((SSD_CHUNK, B_WIDTH), cur),
                   pl.BlockSpec((1, B_HEADS, B_HEAD_DIM, D_STATE), lambda b, c: (b, 0, 0, 0))],
        out_shape=[jax.ShapeDtypeStruct((batch * seq, B_WIDTH), BF16),
                   jax.ShapeDtypeStruct((batch, B_HEADS, B_HEAD_DIM, D_STATE), F32)],
        scratch_shapes=[pltpu.VMEM((8 + SSD_CHUNK, CONV_DIM), F32)],
        compiler_params=_cparams("parallel", "arbitrary"),
        name="ssd_prompt",
    )(d_skip, xbc, dt, z, conv_w, conv_b, dt_bias, a_log, norm_w)


def _ssd_sample_kernel(dskip_ref, xbc_ref, hist_ref, dt_ref, z_ref, h0_ref, cw_ref, cb_ref, dtb_ref,
                       alog_ref, nw_ref, o_ref, conv_o, state_o):
    x = xbc_ref[...]
    hist = hist_ref[...]
    acc = cb_ref[...][None] + x * cw_ref[CONV_WIDTH - 1:CONV_WIDTH, :][None]
    for j in range(CONV_WIDTH - 1):
        acc = acc + hist[:, j:j + 1, :] * cw_ref[j:j + 1, :][None]
    conv_o[:, 0:CONV_WIDTH - 2, :] = hist[:, 1:CONV_WIDTH - 1, :]
    conv_o[:, CONV_WIDTH - 2:CONV_WIDTH - 1, :] = x
    u = _silu(acc)
    xs = u[:, :, :B_WIDTH]
    bm = u[:, :, B_WIDTH:B_WIDTH + B_GROUPS * D_STATE]
    cm = u[:, :, B_WIDTH + B_GROUPS * D_STATE:]
    dt = _softplus(dt_ref[...] + dtb_ref[...][None])
    dec = jnp.exp(dt * (-jnp.exp(alog_ref[...]))[None])
    ys = []
    for h in range(B_HEADS):
        g = h // B_REP
        xs_h = xs[:, :, h * B_HEAD_DIM:(h + 1) * B_HEAD_DIM]
        xdt = xs_h * dt[:, :, h:h + 1]
        bm_g = bm[:, :, g * D_STATE:(g + 1) * D_STATE]
        cm_g = cm[:, :, g * D_STATE:(g + 1) * D_STATE]
        dec_h = dec[:, :, h:h + 1]
        st = h0_ref[:, h]
        cb = jnp.sum(cm_g * bm_g, axis=-1, keepdims=True)
        y_off = jnp.einsum("bqn,bpn->bqp", cm_g.astype(BF16), st.astype(BF16),
                           preferred_element_type=F32)
        ys.append(cb * xdt + y_off * dec_h + dskip_ref[h] * xs_h)
        outer = jnp.einsum("bqp,bqn->bpn", xdt.astype(BF16), bm_g.astype(BF16),
                           preferred_element_type=F32)
        state_o[:, h] = st * dec_h + outer
    y = jnp.concatenate(ys, axis=-1)
    o_ref[...] = _gated_rmsnorm(y, z_ref[...], nw_ref[...][None]).astype(o_ref.dtype)


def _ssd_sample(xbc3, hist, dt3, z3, h0, conv_w, conv_b, dt_bias, a_log, d_skip, norm_w, bb):
    nb = xbc3.shape[0]
    b3 = lambda i: (i, 0, 0)
    b4 = lambda i: (i, 0, 0, 0)
    return pl.pallas_call(
        _ssd_sample_kernel,
        grid=(nb // bb,),
        in_specs=[pl.BlockSpec(memory_space=pltpu.SMEM),
                  pl.BlockSpec((bb, 1, CONV_DIM), b3),
                  pl.BlockSpec((bb, CONV_WIDTH - 1, CONV_DIM), b3),
                  pl.BlockSpec((bb, 1, LANES), b3),
                  pl.BlockSpec((bb, 1, B_WIDTH), b3),
                  pl.BlockSpec((bb, B_HEADS, B_HEAD_DIM, D_STATE), b4),
                  _const_spec(conv_w.shape), _const_spec(conv_b.shape), _const_spec(dt_bias.shape),
                  _const_spec(a_log.shape), _const_spec(norm_w.shape)],
        out_specs=[pl.BlockSpec((bb, 1, B_WIDTH), b3),
                   pl.BlockSpec((bb, CONV_WIDTH - 1, CONV_DIM), b3),
                   pl.BlockSpec((bb, B_HEADS, B_HEAD_DIM, D_STATE), b4)],
        out_shape=[jax.ShapeDtypeStruct((nb, 1, B_WIDTH), F32),
                   jax.ShapeDtypeStruct((nb, CONV_WIDTH - 1, CONV_DIM), F32),
                   jax.ShapeDtypeStruct((nb, B_HEADS, B_HEAD_DIM, D_STATE), F32)],
        compiler_params=_cparams("parallel"),
        name="ssd_sample",
    )(d_skip, xbc3, hist, dt3, z3, h0, conv_w, conv_b, dt_bias, a_log, norm_w)


def _outproj_ln_kernel(n_parts, x_ref, *refs):
    parts, ws = refs[:n_parts], refs[n_parts:2 * n_parts]
    g_ref, b_ref, o_ref = refs[2 * n_parts:]
    mix = _dot(parts[0][...].astype(BF16), ws[0][...])
    for a, w in zip(parts[1:], ws[1:]):
        mix = mix + _dot(a[...].astype(BF16), w[...])
    y = DN_ALPHA * x_ref[...] + mix
    mu = jnp.mean(y, axis=-1, keepdims=True)
    var = jnp.mean(jnp.square(y - mu), axis=-1, keepdims=True)
    o_ref[...] = (y - mu) * lax.rsqrt(var + LN_EPS) * g_ref[...] + b_ref[...]


def _outproj_ln(x2d, parts, ws, ln_g, ln_b, tm):
    n = x2d.shape[0]
    row = lambda i: (i, 0)
    return pl.pallas_call(
        functools.partial(_outproj_ln_kernel, len(parts)),
        grid=(n // tm,),
        in_specs=[pl.BlockSpec((tm, D_MODEL), row)]
                 + [pl.BlockSpec((tm, p.shape[1]), row) for p in parts]
                 + [_const_spec(w.shape) for w in ws]
                 + [_const_spec(ln_g.shape), _const_spec(ln_b.shape)],
        out_specs=pl.BlockSpec((tm, D_MODEL), row),
        out_shape=jax.ShapeDtypeStruct((n, D_MODEL), F32),
        compiler_params=_cparams("parallel"),
        name="outproj_ln",
    )(x2d, *parts, *ws, ln_g, ln_b)


def _bf16_pieces(c):
    pieces, rem = [], c
    for _ in range(C_PIECES):
        p = rem.astype(BF16).astype(F32)
        pieces.append(p)
        rem = rem - p
    return pieces


def _odd_inproj_prompt_kernel(tiles_per_seq, x_ref, wq, wk, wv, wf, wg, fb_ref, sq_ref, sk_ref, cq_ref, ck_ref,
                              qa_o, ka_o, vt_o, kt_o, vtf_o, lf_o, g_o, carry_ref):
    i = pl.program_id(0)
    tm = x_ref.shape[0]

    @pl.when(i % tiles_per_seq == 0)
    def _():
        carry_ref[...] = jnp.zeros_like(carry_ref)

    xb = x_ref[...].astype(BF16)
    kf = _dot(xb, wk[...])
    vf = _dot(xb, wv[...])
    g_o[...] = _dot(xb, wg[...]).astype(g_o.dtype)
    lf = _log_sigmoid(_dot(xb, wf[...]) + fb_ref[...])
    lf_o[...] = lf[:, :C_HEADS]

    c = jnp.where(_lane_iota(lf.shape) < C_HEADS, lf, 0.0)
    rows = lax.broadcasted_iota(jnp.int32, c.shape, 0)
    s = 1
    while s < tm:
        c = c + jnp.where(rows >= s, pltpu.roll(c, s, axis=0), 0.0)
        s *= 2
    c = c + carry_ref[...]
    carry_ref[...] = c[tm - 1:tm, :]

    hi, mid, lo = _bf16_pieces(c * LOG2E)
    c3 = (hi + pltpu.roll(mid, C_HEADS, axis=1) + pltpu.roll(lo, 2 * C_HEADS, axis=1)).astype(BF16)
    ex_q = _dot(c3, sq_ref[...]) + cq_ref[...]
    ex_k = _dot(c3, sk_ref[...]) + ck_ref[...]

    qf = _dot(xb, wq[...])
    for h, t in enumerate(_split_heads_to_lane_tiles(qf, C_WIDTH // LANES)):
        sl = slice(h * LANES, (h + 1) * LANES)
        qa_o[:, sl] = (t + ex_q[:, sl]).astype(qa_o.dtype)
    ones_row = (_lane_iota((1, LANES)) == HEAD_DIM).astype(F32)
    k_tiles = _split_heads_to_lane_tiles(kf, C_KV_WIDTH // LANES)
    v_tiles = _split_heads_to_lane_tiles(vf, C_KV_WIDTH // LANES)
    for g in range(C_KV_HEADS):
        sl = slice(g * LANES, (g + 1) * LANES)
        ka_o[:, sl] = (k_tiles[g] + ex_k[:, sl]).astype(ka_o.dtype)
        v_t = (v_tiles[g] + ones_row).T
        vt_o[0, g] = v_t.astype(vt_o.dtype)
        vtf_o[0, g] = v_t[:HEAD_DIM]
        kt_o[0, g] = k_tiles[g].T[:HEAD_DIM]


def _aug_selectors():
    sq = np.zeros((LANES, C_HEADS * LANES), np.float32)
    sk = np.zeros((LANES, C_KV_HEADS * LANES), np.float32)
    cq = np.zeros((1, C_HEADS * LANES), np.float32)
    ck = np.zeros((1, C_KV_HEADS * LANES), np.float32)
    for h in range(C_HEADS):
        g, r = divmod(h, C_REP)
        for p in range(C_PIECES):
            sq[p * C_HEADS + h, h * LANES + HEAD_DIM + p] = 1.0
            col = HEAD_DIM + C_PIECES + C_PIECES * r + p
            cq[0, h * LANES + col] = -1.0
            sk[p * C_HEADS + h, g * LANES + col] = 1.0
    for g in range(C_KV_HEADS):
        for p in range(C_PIECES):
            ck[0, g * LANES + HEAD_DIM + p] = 1.0
    return jnp.asarray(sq, BF16), jnp.asarray(sk, BF16), jnp.asarray(cq), jnp.asarray(ck)


def _odd_inproj_prompt(x2d, w, f_bias, batch, seq, tm):
    n = x2d.shape[0]
    tiles = seq // tm
    sq, sk, cq, ck = _aug_selectors()
    row = lambda i: (i, 0)
    consts = list(w) + [f_bias, sq, sk, cq, ck]
    outs = [((n, C_HEADS * LANES), BF16), ((n, C_KV_HEADS * LANES), BF16),
            ((batch, C_KV_HEADS, LANES, seq), BF16), ((batch, C_KV_HEADS, HEAD_DIM, seq), F32),
            ((batch, C_KV_HEADS, HEAD_DIM, seq), F32), ((n, C_HEADS), F32), ((n, C_WIDTH), BF16)]
    seq_minor = lambda i: (i // tiles, 0, 0, i % tiles)
    out_specs = [pl.BlockSpec((tm, outs[0][0][1]), row), pl.BlockSpec((tm, outs[1][0][1]), row),
                 pl.BlockSpec((1, C_KV_HEADS, LANES, tm), seq_minor),
                 pl.BlockSpec((1, C_KV_HEADS, HEAD_DIM, tm), seq_minor),
                 pl.BlockSpec((1, C_KV_HEADS, HEAD_DIM, tm), seq_minor),
                 pl.BlockSpec((tm, C_HEADS), row), pl.BlockSpec((tm, C_WIDTH), row)]
    return pl.pallas_call(
        functools.partial(_odd_inproj_prompt_kernel, tiles),
        grid=(n // tm,),
        in_specs=[pl.BlockSpec((tm, D_MODEL), row)] + [_const_spec(a.shape) for a in consts],
        out_specs=out_specs,
        out_shape=[jax.ShapeDtypeStruct(s, d) for s, d in outs],
        scratch_shapes=[pltpu.VMEM((1, LANES), F32)],
        compiler_params=_cparams("arbitrary"),
        name="odd_inproj_prompt",
    )(x2d, *consts)


def _odd_inproj_sample_kernel(x_ref, wq, wk, wv, wf, wg, fb_ref, q_o, k_o, v_o, lf_o, g_o):
    xb = x_ref[...].astype(BF16)
    q_o[...] = _dot(xb, wq[...])
    k_o[...] = _dot(xb, wk[...])
    v_o[...] = _dot(xb, wv[...])
    lf_o[...] = _log_sigmoid(_dot(xb, wf[...]) + fb_ref[...])[:, :C_HEADS]
    g_o[...] = _dot(xb, wg[...])


def _odd_inproj_sample(x2d, w, f_bias):
    n = x2d.shape[0]
    outs = [((n, C_WIDTH), F32), ((n, C_KV_WIDTH), F32), ((n, C_KV_WIDTH), F32), ((n, C_HEADS), F32),
            ((n, C_WIDTH), F32)]
    return pl.pallas_call(
        _odd_inproj_sample_kernel,
        out_shape=[jax.ShapeDtypeStruct(s, d) for s, d in outs],
        compiler_params=pltpu.CompilerParams(vmem_limit_bytes=VMEM_LIMIT),
        name="odd_inproj_sample",
    )(x2d, *w, f_bias)


def _fox_prompt_kernel(q_ref, k_ref, vt_ref, g_ref, o_ref, qs_ref, m_ref, alpha_ref, acc_ref, p_ref, s_ref):
    qi = pl.program_id(1)
    tq, tk = FOX_TQ, FOX_TK
    for g in range(C_KV_HEADS):
        for r in range(C_REP):
            h = g * C_REP + r
            qs_ref[g, r * tq:(r + 1) * tq, :] = q_ref[:, h * LANES:(h + 1) * LANES]
    m_ref[...] = jnp.full(m_ref.shape, -jnp.inf, F32)
    acc_ref[...] = jnp.zeros_like(acc_ref)
    n_last = (qi * tq) // tk

    def qk(t, slot, masked):
        start = t * tk if isinstance(t, int) else pl.multiple_of(t * tk, tk)
        for g in range(C_KV_HEADS):
            k_t = k_ref[pl.ds(start, tk), g * LANES:(g + 1) * LANES]
            s_t = _dot_nt(k_t, qs_ref[g])
            if masked:
                krow = lax.broadcasted_iota(jnp.int32, (tk, 1), 0)
                qrel = (qi * tq - start) + _lane_iota((1, C_REP * tq)) % tq
                s_t = jnp.where(krow <= qrel, s_t, -jnp.inf)
            s_ref[slot, g] = s_t

    def softmax(slot):
        for g in range(C_KV_HEADS):
            s_t = s_ref[slot, g]
            m_old = m_ref[g]
            m_new = jnp.maximum(m_old, jnp.max(s_t, axis=0, keepdims=True))
            p_ref[slot, g] = jnp.exp2(s_t - m_new).astype(BF16)
            alpha_ref[slot, g] = jnp.exp2(m_old - m_new)
            m_ref[g] = m_new

    def pv(t, slot):
        start = t * tk if isinstance(t, int) else pl.multiple_of(t * tk, tk)
        for g in range(C_KV_HEADS):
            acc_ref[g] = (acc_ref[g] * alpha_ref[slot, g]
                          + _dot(vt_ref[0, g, :, pl.ds(start, tk)], p_ref[slot, g]))

    def guarded_step(tau, parity):
        @pl.when(tau <= n_last)
        def _():
            qk(tau, parity, True)

        @pl.when((tau >= 1) & (tau <= n_last + 1))
        def _():
            softmax(1 - parity)

        @pl.when((tau >= 2) & (tau <= n_last + 2))
        def _():
            pv(tau - 2, parity)

    def steady_pair(i, carry):
        tau = 2 + 2 * i
        qk(tau, 0, False)
        softmax(1)
        pv(tau - 2, 0)
        qk(tau + 1, 1, False)
        softmax(0)
        pv(tau - 1, 1)
        return carry

    @pl.when(n_last < 2)
    def _():
        for tau in range(4):
            guarded_step(tau, tau % 2)

    @pl.when(n_last >= 2)
    def _():
        qk(0, 0, False)
        qk(1, 1, False)
        softmax(0)
        n_pairs = (n_last - 2) // 2
        lax.fori_loop(0, n_pairs, steady_pair, 0)

        @pl.when(n_last % 2 == 0)
        def _():
            qk(n_last, 0, True)
            softmax(1)
            pv(n_last - 2, 0)
            softmax(0)
            pv(n_last - 1, 1)
            pv(n_last, 0)

        @pl.when(n_last % 2 == 1)
        def _():
            qk(n_last - 1, 0, False)
            softmax(1)
            pv(n_last - 3, 0)
            qk(n_last, 1, True)
            softmax(0)
            pv(n_last - 2, 1)
            softmax(1)
            pv(n_last - 1, 0)
            pv(n_last, 1)

    tiles = []
    for g in range(C_KV_HEADS):
        acc = acc_ref[g]
        o_t = acc[:HEAD_DIM] / acc[HEAD_DIM:HEAD_DIM + 1]
        tiles.extend(o_t[:, r * tq:(r + 1) * tq].T for r in range(C_REP))
    o = jnp.concatenate(tiles, axis=1)
    o_ref[...] = (o * _silu(g_ref[...].astype(F32))).astype(o_ref.dtype)


def _fox_prompt(q_aug, k_aug, vt_aug, g, batch, seq):
    nq = seq // FOX_TQ
    row = lambda b, i: (b * nq + i, 0)
    once = pl.Buffered(1)
    return pl.pallas_call(
        _fox_prompt_kernel,
        grid=(batch, nq),
        in_specs=[pl.BlockSpec((FOX_TQ, C_HEADS * LANES), row),
                  pl.BlockSpec((seq, C_KV_HEADS * LANES), lambda b, i: (b, 0), pipeline_mode=once),
                  pl.BlockSpec((1, C_KV_HEADS, LANES, seq), lambda b, i: (b, 0, 0, 0), pipeline_mode=once),
                  pl.BlockSpec((FOX_TQ, C_WIDTH), row)],
        out_specs=pl.BlockSpec((FOX_TQ, C_WIDTH), row),
        out_shape=jax.ShapeDtypeStruct((batch * seq, C_WIDTH), BF16),
        scratch_shapes=[pltpu.VMEM((C_KV_HEADS, C_REP * FOX_TQ, LANES), BF16),
                        pltpu.VMEM((C_KV_HEADS, 1, C_REP * FOX_TQ), F32),
                        pltpu.VMEM((2, C_KV_HEADS, 1, C_REP * FOX_TQ), F32),
                        pltpu.VMEM((C_KV_HEADS, LANES, C_REP * FOX_TQ), F32),
                        pltpu.VMEM((2, C_KV_HEADS, FOX_TK, C_REP * FOX_TQ), BF16),
                        pltpu.VMEM((2, C_KV_HEADS, FOX_TK, C_REP * FOX_TQ), F32)],
        compiler_params=_cparams("parallel", "arbitrary"),
        name="fox_prompt",
    )(q_aug, k_aug, vt_aug, g)


def _fox_decode_kernel(pt_ref, q_ref, kn_ref, vn_ref, lfn_ref, g_ref, k_hbm, v_hbm, lf_hbm, o_ref,
                       kbuf, vbuf, lfbuf, sem, qm_ref, m_ref, l_ref, acc_ref, carry_ref):
    b, c = pl.program_id(0), pl.program_id(1)
    nb, nc = pl.num_programs(0), pl.num_programs(1)
    G = DEC_PAGES
    n_pages = nc * G
    step = b * nc + c
    slot = step % 2

    def copies(bb, cc, sl):
        out = []
        for g in range(G):
            page = pt_ref[bb, n_pages - (cc + 1) * G + g]
            out.append((pltpu.make_async_copy(k_hbm.at[page], kbuf.at[sl, g], sem.at[0, sl]), g % 2))
            out.append((pltpu.make_async_copy(v_hbm.at[page], vbuf.at[sl, g], sem.at[1, sl]), (g + 1) % 2))
            out.append((pltpu.make_async_copy(lf_hbm.at[page], lfbuf.at[sl, g], sem.at[2, sl]), 0))
        return out

    @pl.when(step == 0)
    def _():
        for cp, prio in copies(0, 0, 0):
            cp.start(priority=prio)

    @pl.when(step + 1 < nb * nc)
    def _():
        nxt = step + 1
        for cp, prio in copies(nxt // nc, nxt % nc, 1 - slot):
            cp.start(priority=prio)

    @pl.when(c == 0)
    def _():
        q16 = q_ref[0]
        q4 = jnp.concatenate([q16] * C_KV_HEADS, axis=1)
        hrow = lax.broadcasted_iota(jnp.int32, q4.shape, 0) // C_REP
        gcol = _lane_iota(q4.shape) // HEAD_DIM
        qm = jnp.where(hrow == gcol, q4, 0.0).astype(BF16)
        qm_ref[...] = qm
        kn = kn_ref[0].astype(BF16).astype(F32)
        m_ref[...] = jnp.sum(qm.astype(F32) * kn, axis=-1, keepdims=True)
        l_ref[...] = jnp.ones_like(l_ref)
        acc_ref[...] = jnp.broadcast_to(vn_ref[0].astype(BF16).astype(F32), acc_ref.shape)
        carry_ref[...] = lfn_ref[0]

    for cp, _ in copies(b, c, slot):
        cp.wait()

    lf_rows = lfbuf[slot].reshape(G * C_HEADS, PAGE_SIZE)
    hi = lf_rows.astype(BF16)
    lo = (lf_rows - hi.astype(F32)).astype(BF16)
    ii = lax.broadcasted_iota(jnp.int32, (PAGE_SIZE, PAGE_SIZE), 0)
    jj = lax.broadcasted_iota(jnp.int32, (PAGE_SIZE, PAGE_SIZE), 1)
    later = (ii > jj).astype(BF16)
    within = _dot(hi, later) + _dot(lo, later)
    carry = carry_ref[...]
    cols = [None] * G
    for g in reversed(range(G)):
        rs = slice(g * C_HEADS, (g + 1) * C_HEADS)
        cols[g] = within[rs] + carry
        carry = carry + within[rs][:, 0:1] + lf_rows[rs][:, 0:1]
    carry_ref[...] = carry

    kc = jnp.concatenate([kbuf[slot, g].astype(BF16) for g in range(G)], axis=1)
    s = _dot(qm_ref[...], kc) + jnp.concatenate(cols, axis=1)
    m_old = m_ref[...]
    m_new = jnp.maximum(m_old, jnp.max(s, axis=-1, keepdims=True))
    alpha = jnp.exp(m_old - m_new)
    p = jnp.exp(s - m_new)
    l_ref[...] = l_ref[...] * alpha + jnp.sum(p, axis=-1, keepdims=True)
    vc = jnp.concatenate([vbuf[slot, g].astype(BF16) for g in range(G)], axis=1)
    acc_ref[...] = acc_ref[...] * alpha + _dot_nt(p.astype(BF16), vc)
    m_ref[...] = m_new

    @pl.when(c == nc - 1)
    def _():
        o_all = acc_ref[...] / l_ref[...]
        hrow = lax.broadcasted_iota(jnp.int32, (C_HEADS, HEAD_DIM), 0) // C_REP
        o = jnp.zeros((C_HEADS, HEAD_DIM), F32)
        for g in range(C_KV_HEADS):
            o = jnp.where(hrow == g, o_all[:, g * HEAD_DIM:(g + 1) * HEAD_DIM], o)
        o_ref[0] = (o * _silu(g_ref[0])).astype(o_ref.dtype)


def _fox_decode(page_table, q3, k_new, v_new, lf_new, g3, cache_k, cache_v, cache_lf):
    nb, n_pages = page_table.shape
    nc = n_pages // DEC_PAGES
    b3 = lambda b, c, pt: (b, 0, 0)
    grid_spec = pltpu.PrefetchScalarGridSpec(
        num_scalar_prefetch=1,
        grid=(nb, nc),
        in_specs=[pl.BlockSpec((1, C_HEADS, HEAD_DIM), b3),
                  pl.BlockSpec((1, 1, C_KV_WIDTH), b3),
                  pl.BlockSpec((1, 1, C_KV_WIDTH), b3),
                  pl.BlockSpec((1, C_HEADS, 1), b3),
                  pl.BlockSpec((1, C_HEADS, HEAD_DIM), b3),
                  pl.BlockSpec(memory_space=pl.ANY),
                  pl.BlockSpec(memory_space=pl.ANY),
                  pl.BlockSpec(memory_space=pl.ANY)],
        out_specs=pl.BlockSpec((1, C_HEADS, HEAD_DIM), b3),
        scratch_shapes=[pltpu.VMEM((2, DEC_PAGES, C_KV_WIDTH, PAGE_SIZE), F32),
                        pltpu.VMEM((2, DEC_PAGES, C_KV_WIDTH, PAGE_SIZE), F32),
                        pltpu.VMEM((2, DEC_PAGES, C_HEADS, PAGE_SIZE), F32),
                        pltpu.SemaphoreType.DMA((3, 2)),
                        pltpu.VMEM((C_HEADS, C_KV_WIDTH), BF16),
                        pltpu.VMEM((C_HEADS, 1), F32),
                        pltpu.VMEM((C_HEADS, 1), F32),
                        pltpu.VMEM((C_HEADS, C_KV_WIDTH), F32),
                        pltpu.VMEM((C_HEADS, 1), F32)])
    return pl.pallas_call(
        _fox_decode_kernel,
        grid_spec=grid_spec,
        out_shape=jax.ShapeDtypeStruct((nb, C_HEADS, HEAD_DIM), BF16),
        compiler_params=_cparams("arbitrary", "arbitrary"),
        name="fox_decode",
    )(page_table, q3, k_new, v_new, lf_new, g3, cache_k, cache_v, cache_lf)


def _split_cols(w, sizes):
    offs = np.cumsum(sizes)[:-1].tolist()
    return jnp.split(w, offs, axis=-1)


def _pad_lanes(a):
    return jnp.pad(a, ((0, 0), (0, LANES - a.shape[-1])))


def _rope_tables(pos):
    half = HEAD_DIM // 2
    inv_freq = ROPE_THETA ** (-jnp.arange(half, dtype=F32) / half)
    ang = pos.astype(F32)[:, None] * inv_freq[None, :]
    cos, sin = jnp.cos(ang), jnp.sin(ang)
    reps = LANES // HEAD_DIM
    return (jnp.tile(jnp.concatenate([cos, cos], axis=1), (1, reps)),
            jnp.tile(jnp.concatenate([-sin, sin], axis=1), (1, reps)))


def _even_weights(w_in, conv_w, conv_b, dt_bias, a_log, ssm_norm_w, w_out, ln_g, ln_b):
    wq, wk, wv, wg, wz, wx, wdt = _split_cols(w_in, EVEN_SPLITS)
    proj = [a.astype(BF16) for a in (wq, wk, wv, wg, wz, wx, _pad_lanes(wdt))]
    wo = w_out.astype(BF16)
    return dict(proj=proj, conv_w=conv_w, conv_b=conv_b[None], dt_bias=_pad_lanes(dt_bias[None]),
                a_log=_pad_lanes(a_log[None]), norm_w=ssm_norm_w[None],
                wo=(wo[:A_WIDTH], wo[A_WIDTH:]), ln_g=ln_g[None], ln_b=ln_b[None])


def _odd_weights(w_in, f_bias, w_out, ln_g, ln_b):
    wq, wk, wv, wf, wg = _split_cols(w_in, ODD_SPLITS)
    proj = [a.astype(BF16) for a in (wq * ATTN_SCALE, wk, wv, _pad_lanes(wf), wg)]
    proj_log2 = [(wq * (ATTN_SCALE * LOG2E)).astype(BF16)] + proj[1:]
    return dict(proj=proj, proj_log2=proj_log2, f_bias=_pad_lanes(f_bias[None]), wo=(w_out.astype(BF16),),
                ln_g=ln_g[None], ln_b=ln_b[None])


def _even_prompt(x, ew, sinks, d_skip):
    b, t, _ = x.shape
    x2 = x.reshape(b * t, D_MODEL)
    cos, sin = _rope_tables(jnp.arange(t, dtype=jnp.int32))
    q, k, v, g, z, xbc, dt = _even_inproj(x2, cos, sin, ew["proj"], tm=1024, act_dtype=BF16)
    o_a = _swa_prompt(q, k, v, g, sinks, b, t)
    o_b, state = _ssd_prompt(xbc, dt, z, ew["conv_w"], ew["conv_b"], ew["dt_bias"], ew["a_log"], d_skip,
                             ew["norm_w"], b, t)
    y = _outproj_ln(x2, (o_a, o_b), ew["wo"], ew["ln_g"], ew["ln_b"], tm=512)
    new_k = k.reshape(b, t, A_KV_WIDTH)[:, -WINDOW:].reshape(b, WINDOW, A_KV_HEADS, HEAD_DIM)
    new_v = v.reshape(b, t, A_KV_WIDTH)[:, -WINDOW:].reshape(b, WINDOW, A_KV_HEADS, HEAD_DIM)
    new_conv = xbc.reshape(b, t, CONV_DIM)[:, -(CONV_WIDTH - 1):]
    return y.reshape(b, t, D_MODEL), new_k, new_v, new_conv, state


def _even_sample(x, pos, ew, sinks, d_skip, swa_k, swa_v, conv_hist, ssm_h0):
    b, t, _ = x.shape
    x2 = x.reshape(b, D_MODEL)
    cos, sin = _rope_tables(jnp.full((b,), pos, jnp.int32))
    q, k, v, g, z, xbc, dt = _even_inproj(x2, cos, sin, ew["proj"], tm=b, act_dtype=F32)
    o_a, new_k, new_v = _swa_sample(
        q.reshape(b, A_HEADS, HEAD_DIM), k.reshape(b, 1, A_KV_WIDTH), v.reshape(b, 1, A_KV_WIDTH),
        swa_k.reshape(b, WINDOW, A_KV_WIDTH), swa_v.reshape(b, WINDOW, A_KV_WIDTH),
        g.reshape(b, A_HEADS, HEAD_DIM), sinks, bb=16)
    o_b, new_conv, new_state = _ssd_sample(xbc.reshape(b, 1, CONV_DIM), conv_hist, dt.reshape(b, 1, LANES),
                                           z.reshape(b, 1, B_WIDTH), ssm_h0,
                                           ew["conv_w"], ew["conv_b"], ew["dt_bias"], ew["a_log"], d_skip,
                                           ew["norm_w"], bb=8)
    y = _outproj_ln(x2, (o_a.reshape(b, A_WIDTH), o_b.reshape(b, B_WIDTH)), ew["wo"], ew["ln_g"], ew["ln_b"],
                    tm=b)
    return (y.reshape(b, 1, D_MODEL), new_k.reshape(b, WINDOW, A_KV_HEADS, HEAD_DIM),
            new_v.reshape(b, WINDOW, A_KV_HEADS, HEAD_DIM), new_conv, new_state)


def _odd_prompt(x, ow):
    b, t, _ = x.shape
    x2 = x.reshape(b * t, D_MODEL)
    q_aug, k_aug, vt_aug, kt, vt, lf, g = _odd_inproj_prompt(x2, ow["proj_log2"], ow["f_bias"], b, t, tm=512)
    o = _fox_prompt(q_aug, k_aug, vt_aug, g, b, t)
    y = _outproj_ln(x2, (o,), ow["wo"], ow["ln_g"], ow["ln_b"], tm=512)
    return (y.reshape(b, t, D_MODEL), jnp.transpose(kt, (0, 3, 1, 2)), jnp.transpose(vt, (0, 3, 1, 2)),
            lf.reshape(b, t, C_HEADS))


def _odd_sample(x, ow, cache_k, cache_v, cache_lf, page_table):
    b, t, _ = x.shape
    x2 = x.reshape(b, D_MODEL)
    q, k, v, lf, g = _odd_inproj_sample(x2, ow["proj"], ow["f_bias"])
    n_phys = cache_k.shape[0]
    kt = jnp.transpose(cache_k, (0, 2, 3, 1)).reshape(n_phys, C_KV_WIDTH, PAGE_SIZE)
    vt = jnp.transpose(cache_v, (0, 2, 3, 1)).reshape(n_phys, C_KV_WIDTH, PAGE_SIZE)
    lft = jnp.swapaxes(cache_lf, 1, 2)
    o = _fox_decode(page_table, q.reshape(b, C_HEADS, HEAD_DIM), k.reshape(b, 1, C_KV_WIDTH),
                    v.reshape(b, 1, C_KV_WIDTH), lf.reshape(b, C_HEADS, 1), g.reshape(b, C_HEADS, HEAD_DIM),
                    kt, vt, lft)
    y = _outproj_ln(x2, (o.reshape(b, C_WIDTH),), ow["wo"], ow["ln_g"], ow["ln_b"], tm=b)
    return (y.reshape(b, 1, D_MODEL), k.reshape(b, 1, C_KV_HEADS, HEAD_DIM),
            v.reshape(b, 1, C_KV_HEADS, HEAD_DIM), lf.reshape(b, 1, C_HEADS))


def kernel(x_prompt, x_sample, cache_swa_k, cache_swa_v, state_conv, state_ssm, cache_fox_k, cache_fox_v, cache_fox_logf, page_table, w_in_even, attn_sinks, conv_w, conv_b, dt_bias, a_log, d_skip, ssm_norm_w, w_out_even, ln_g_even, ln_b_even, w_in_odd, forget_bias, w_out_odd, ln_g_odd, ln_b_odd):
    past_len = page_table.shape[1] * PAGE_SIZE
    ew = _even_weights(w_in_even[0], conv_w[0], conv_b[0], dt_bias[0], a_log[0], ssm_norm_w[0],
                       w_out_even[0], ln_g_even[0], ln_b_even[0])
    ow = _odd_weights(w_in_odd[0], forget_bias[0], w_out_odd[0], ln_g_odd[0], ln_b_odd[0])

    yp, swa_kp, swa_vp, conv_p, ssm_p = _even_prompt(x_prompt, ew, attn_sinks[0], d_skip[0])
    ys, swa_ks, swa_vs, conv_s, ssm_s = _even_sample(x_sample, past_len, ew, attn_sinks[0], d_skip[0],
                                                     cache_swa_k[0], cache_swa_v[0], state_conv[0], state_ssm[0])
    yp, fox_kp, fox_vp, fox_lp = _odd_prompt(yp, ow)
    ys, fox_ks, fox_vs, fox_ls = _odd_sample(ys, ow, cache_fox_k[0], cache_fox_v[0], cache_fox_logf[0], page_table)

    one = lambda a: a[None]
    return (yp, ys, one(swa_kp), one(swa_vp), one(swa_ks), one(swa_vs), one(conv_p), one(conv_s),
            one(ssm_p), one(ssm_s), one(fox_kp), one(fox_vp), one(fox_lp), one(fox_ks), one(fox_vs), one(fox_ls))
```

```python
import functools

import numpy as np
import jax
import jax.numpy as jnp
from jax import lax
from jax.experimental import pallas as pl
from jax.experimental.pallas import tpu as pltpu

F32 = jnp.float32
BF16 = jnp.bfloat16

D_MODEL = 1024
DEPTH = 2
HEAD_DIM = 64
ATTN_SCALE = HEAD_DIM ** -0.5
A_HEADS, A_KV_HEADS = 8, 2
A_REP = A_HEADS // A_KV_HEADS
A_WIDTH, A_KV_WIDTH = A_HEADS * HEAD_DIM, A_KV_HEADS * HEAD_DIM
WINDOW = 128
ROPE_THETA = 10000.0
B_HEADS, B_HEAD_DIM, B_GROUPS = 8, 64, 2
B_REP = B_HEADS // B_GROUPS
B_WIDTH = B_HEADS * B_HEAD_DIM
D_STATE = 128
CONV_WIDTH = 4
CONV_DIM = B_WIDTH + 2 * B_GROUPS * D_STATE
SSD_CHUNK = 128
C_HEADS, C_KV_HEADS = 16, 4
C_REP = C_HEADS // C_KV_HEADS
C_WIDTH, C_KV_WIDTH = C_HEADS * HEAD_DIM, C_KV_HEADS * HEAD_DIM
PAGE_SIZE = 128
DN_ALPHA = (2 * DEPTH) ** 0.25
LN_EPS = 1e-5
RMS_EPS = 1e-5
EVEN_SPLITS = (A_WIDTH, A_KV_WIDTH, A_KV_WIDTH, A_WIDTH, B_WIDTH, CONV_DIM, B_HEADS)
ODD_SPLITS = (C_WIDTH, C_KV_WIDTH, C_KV_WIDTH, C_HEADS, C_WIDTH)

LANES = 128
VMEM_LIMIT = 56 * 1024 * 1024
C_PIECES = 3
FOX_TQ = 128
LOG2E = 1.4426950408889634
FOX_TK = 512
DEC_PAGES = 64


def _cparams(*sem):
    return pltpu.CompilerParams(dimension_semantics=sem, vmem_limit_bytes=VMEM_LIMIT)


def _const_spec(shape):
    nd = len(shape)
    return pl.BlockSpec(shape, lambda *_: (0,) * nd, pipeline_mode=pl.Buffered(1))


def _silu(x):
    return x * (1.0 / (1.0 + jnp.exp(-x)))


def _softplus(x):
    return jnp.maximum(x, 0.0) + jnp.log1p(jnp.exp(-jnp.abs(x)))


def _log_sigmoid(x):
    return jnp.minimum(x, 0.0) - jnp.log1p(jnp.exp(-jnp.abs(x)))


def _lane_iota(shape):
    return lax.broadcasted_iota(jnp.int32, shape, len(shape) - 1)


def _dot(a, b):
    return jnp.dot(a, b, preferred_element_type=F32)


def _dot_nt(a, b):
    return lax.dot_general(a, b, (((1,), (1,)), ((), ())), preferred_element_type=F32)


def _dot_tn(a, b):
    return lax.dot_general(a, b, (((0,), (0,)), ((), ())), preferred_element_type=F32)


def _split_heads_to_lane_tiles(x, n_pairs):
    out = []
    for j in range(n_pairs):
        col = x[:, j * LANES:(j + 1) * LANES]
        low = _lane_iota(col.shape) < HEAD_DIM
        out.append(jnp.where(low, col, 0.0))
        out.append(jnp.where(low, pltpu.roll(col, HEAD_DIM, axis=1), 0.0))
    return out


def _rope_cols(x, cos, sin_signed):
    half = HEAD_DIM // 2
    first = (_lane_iota(x.shape) % HEAD_DIM) < half
    swapped = jnp.where(first, pltpu.roll(x, LANES - half, axis=1), pltpu.roll(x, half, axis=1))
    return x * cos + swapped * sin_signed


def _even_inproj_kernel(x_ref, cos_ref, sin_ref, wq, wk, wv, wg, wz, wx, wdt,
                        q_o, k_o, v_o, g_o, z_o, xbc_o, dt_o):
    xb = x_ref[...].astype(BF16)
    cos, sin = cos_ref[...], sin_ref[...]
    q = _dot(xb, wq[...])
    for j in range(A_WIDTH // LANES):
        sl = slice(j * LANES, (j + 1) * LANES)
        q_o[:, sl] = (_rope_cols(q[:, sl], cos, sin) * ATTN_SCALE).astype(q_o.dtype)
    k_o[...] = _rope_cols(_dot(xb, wk[...]), cos, sin)
    v_o[...] = _dot(xb, wv[...])
    g_o[...] = _dot(xb, wg[...]).astype(g_o.dtype)
    z_o[...] = _dot(xb, wz[...]).astype(z_o.dtype)
    xbc_o[...] = _dot(xb, wx[...])
    dt_o[...] = _dot(xb, wdt[...])


def _even_inproj(x2d, cos_tab, sin_tab, w, tm, act_dtype):
    n = x2d.shape[0]
    nblk_seq = cos_tab.shape[0] // tm
    row = lambda i: (i, 0)
    outs = [((n, A_WIDTH), act_dtype), ((n, A_KV_WIDTH), F32), ((n, A_KV_WIDTH), F32),
            ((n, A_WIDTH), act_dtype), ((n, B_WIDTH), act_dtype), ((n, CONV_DIM), F32), ((n, LANES), F32)]
    return pl.pallas_call(
        _even_inproj_kernel,
        grid=(n // tm,),
        in_specs=[pl.BlockSpec((tm, D_MODEL), row),
                  pl.BlockSpec((tm, LANES), lambda i: (i % nblk_seq, 0)),
                  pl.BlockSpec((tm, LANES), lambda i: (i % nblk_seq, 0))]
                 + [_const_spec(a.shape) for a in w],
        out_specs=[pl.BlockSpec((tm, s[1]), row) for s, _ in outs],
        out_shape=[jax.ShapeDtypeStruct(s, d) for s, d in outs],
        compiler_params=_cparams("parallel"),
        name="even_inproj",
    )(x2d, cos_tab, sin_tab, *w)


def _swa_prompt_kernel(sinks_ref, q_ref, kp_ref, kc_ref, vp_ref, vc_ref, g_ref, o_ref):
    n = pl.program_id(1)
    q = q_ref[...]
    kk = jnp.concatenate([kp_ref[...], kc_ref[...]], axis=0).astype(BF16)
    vv = jnp.concatenate([vp_ref[...], vc_ref[...]], axis=0).astype(BF16)
    qpos = WINDOW + lax.broadcasted_iota(jnp.int32, (WINDOW, 2 * WINDOW), 0)
    kpos = lax.broadcasted_iota(jnp.int32, (WINDOW, 2 * WINDOW), 1)
    diff = qpos - kpos
    band = (diff >= 0) & (diff <= WINDOW) & ((kpos >= WINDOW) | (n > 0))
    bias = jnp.concatenate([jnp.where(band, 0.0, -jnp.inf)] * A_REP, axis=0)
    head_of_row = lax.broadcasted_iota(jnp.int32, (A_REP * WINDOW, 1), 0) // WINDOW
    heads = []
    for kv in range(A_KV_HEADS):
        q_stack = jnp.concatenate([q[:, h * HEAD_DIM:(h + 1) * HEAD_DIM]
                                   for h in range(kv * A_REP, (kv + 1) * A_REP)], axis=0)
        kh = kk[:, kv * HEAD_DIM:(kv + 1) * HEAD_DIM]
        vh = vv[:, kv * HEAD_DIM:(kv + 1) * HEAD_DIM]
        s = _dot_nt(q_stack, kh) + bias
        sink = jnp.zeros((A_REP * WINDOW, 1), F32)
        for r in range(A_REP):
            sink = jnp.where(head_of_row == r, sinks_ref[kv * A_REP + r], sink)
        m = jnp.maximum(jnp.max(s, axis=-1, keepdims=True), sink)
        p = jnp.exp(s - m)
        den = jnp.sum(p, axis=-1, keepdims=True) + jnp.exp(sink - m)
        o_stack = _dot(p.astype(BF16), vh) / den
        heads.extend(o_stack[r * WINDOW:(r + 1) * WINDOW] for r in range(A_REP))
    o = jnp.concatenate(heads, axis=1)
    o_ref[...] = (o * _silu(g_ref[...].astype(F32))).astype(o_ref.dtype)


def _swa_prompt(q, k, v, g, sinks, batch, seq):
    nb = seq // WINDOW
    cur = lambda b, n: (b * nb + n, 0)
    prev = lambda b, n: (b * nb + jnp.maximum(n - 1, 0), 0)
    return pl.pallas_call(
        _swa_prompt_kernel,
        grid=(batch, nb),
        in_specs=[pl.BlockSpec(memory_space=pltpu.SMEM),
                  pl.BlockSpec((WINDOW, A_WIDTH), cur),
                  pl.BlockSpec((WINDOW, A_KV_WIDTH), prev),
                  pl.BlockSpec((WINDOW, A_KV_WIDTH), cur),
                  pl.BlockSpec((WINDOW, A_KV_WIDTH), prev),
                  pl.BlockSpec((WINDOW, A_KV_WIDTH), cur),
                  pl.BlockSpec((WINDOW, A_WIDTH), cur)],
        out_specs=pl.BlockSpec((WINDOW, A_WIDTH), cur),
        out_shape=jax.ShapeDtypeStruct((batch * seq, A_WIDTH), BF16),
        compiler_params=_cparams("parallel", "parallel"),
        name="swa_prompt",
    )(sinks, q, k, k, v, v, g)


def _swa_sample_kernel(sinks_ref, q_ref, kn_ref, vn_ref, ck_ref, cv_ref, g_ref, o_ref, nk_ref, nv_ref):
    ck, cv = ck_ref[...], cv_ref[...]
    kn, vn = kn_ref[...], vn_ref[...]
    nk_ref[:, 0:WINDOW - 1, :] = ck[:, 1:WINDOW, :]
    nk_ref[:, WINDOW - 1:WINDOW, :] = kn
    nv_ref[:, 0:WINDOW - 1, :] = cv[:, 1:WINDOW, :]
    nv_ref[:, WINDOW - 1:WINDOW, :] = vn
    q = q_ref[...]
    g = g_ref[...]
    hrow = lax.broadcasted_iota(jnp.int32, (A_REP, 1), 0)
    for kv in range(A_KV_HEADS):
        hs = slice(kv * A_REP, (kv + 1) * A_REP)
        ds = slice(kv * HEAD_DIM, (kv + 1) * HEAD_DIM)
        qh = q[:, hs, :].astype(BF16)
        kh, vh = ck[:, :, ds].astype(BF16), cv[:, :, ds].astype(BF16)
        knh, vnh = kn[:, :, ds].astype(BF16).astype(F32), vn[:, :, ds].astype(BF16).astype(F32)
        s_c = jnp.einsum("bqd,bkd->bqk", qh, kh, preferred_element_type=F32)
        s_n = jnp.sum(qh.astype(F32) * knh, axis=-1, keepdims=True)
        sink = jnp.zeros((A_REP, 1), F32)
        for r in range(A_REP):
            sink = jnp.where(hrow == r, sinks_ref[kv * A_REP + r], sink)
        sink = sink[None]
        m = jnp.maximum(jnp.maximum(jnp.max(s_c, axis=-1, keepdims=True), s_n), sink)
        p_c = jnp.exp(s_c - m)
        p_n = jnp.exp(s_n - m)
        den = jnp.sum(p_c, axis=-1, keepdims=True) + p_n + jnp.exp(sink - m)
        p_c = (p_c / den).astype(BF16)
        p_n = (p_n / den).astype(BF16).astype(F32)
        o = jnp.einsum("bqk,bkd->bqd", p_c, vh, preferred_element_type=F32) + p_n * vnh
        o_ref[:, hs, :] = (o * _silu(g[:, hs, :])).astype(o_ref.dtype)


def _swa_sample(q3, k_new, v_new, cache_k, cache_v, g3, sinks, bb):
    nb = q3.shape[0]
    blk3 = lambda i: (i, 0, 0)
    return pl.pallas_call(
        _swa_sample_kernel,
        grid=(nb // bb,),
        in_specs=[pl.BlockSpec(memory_space=pltpu.SMEM),
                  pl.BlockSpec((bb, A_HEADS, HEAD_DIM), blk3),
                  pl.BlockSpec((bb, 1, A_KV_WIDTH), blk3),
                  pl.BlockSpec((bb, 1, A_KV_WIDTH), blk3),
                  pl.BlockSpec((bb, WINDOW, A_KV_WIDTH), blk3),
                  pl.BlockSpec((bb, WINDOW, A_KV_WIDTH), blk3),
                  pl.BlockSpec((bb, A_HEADS, HEAD_DIM), blk3)],
        out_specs=[pl.BlockSpec((bb, A_HEADS, HEAD_DIM), blk3),
                   pl.BlockSpec((bb, WINDOW, A_KV_WIDTH), blk3),
                   pl.BlockSpec((bb, WINDOW, A_KV_WIDTH), blk3)],
        out_shape=[jax.ShapeDtypeStruct((nb, A_HEADS, HEAD_DIM), F32),
                   jax.ShapeDtypeStruct((nb, WINDOW, A_KV_WIDTH), F32),
                   jax.ShapeDtypeStruct((nb, WINDOW, A_KV_WIDTH), F32)],
        compiler_params=_cparams("parallel"),
        name="swa_sample",
    )(sinks, q3, k_new, v_new, cache_k, cache_v, g3)


def _gated_rmsnorm(y, z, w):
    g = y * _silu(z)
    gw = B_WIDTH // B_GROUPS
    outs = []
    for i in range(B_GROUPS):
        gi = g[..., i * gw:(i + 1) * gw]
        outs.append(gi * lax.rsqrt(jnp.mean(gi * gi, axis=-1, keepdims=True) + RMS_EPS))
    return jnp.concatenate(outs, axis=-1) * w


def _ssd_prompt_kernel(dskip_ref, xbc_ref, dt_ref, z_ref, cw_ref, cb_ref, dtb_ref, alog_ref, nw_ref,
                       o_ref, state_ref, xp_ref):
    L = SSD_CHUNK
    c = pl.program_id(1)

    @pl.when(c == 0)
    def _():
        state_ref[...] = jnp.zeros_like(state_ref)
        xp_ref[0:8, :] = jnp.zeros((8, CONV_DIM), F32)

    x = xbc_ref[...]
    xp_ref[8:8 + L, :] = x
    acc = cb_ref[...] + xp_ref[pl.ds(8 - (CONV_WIDTH - 1), L), :] * cw_ref[0:1, :]
    for j in range(1, CONV_WIDTH):
        acc = acc + xp_ref[pl.ds(8 - (CONV_WIDTH - 1) + j, L), :] * cw_ref[j:j + 1, :]
    xp_ref[0:8, :] = x[L - 8:L, :]
    u = _silu(acc)
    xs = u[:, :B_WIDTH]
    bm = u[:, B_WIDTH:B_WIDTH + B_GROUPS * D_STATE].astype(BF16)
    cm = u[:, B_WIDTH + B_GROUPS * D_STATE:].astype(BF16)

    dt = _softplus(dt_ref[...] + dtb_ref[...])
    da = dt * (-jnp.exp(alog_ref[...]))
    row = lax.broadcasted_iota(jnp.int32, (L, L), 0)
    col = lax.broadcasted_iota(jnp.int32, (L, L), 1)
    causal = row >= col
    cum = jnp.dot(causal.astype(F32), da, precision=lax.Precision.HIGHEST, preferred_element_type=F32)
    cum_t = cum.T
    cb = [_dot_nt(cm[:, g * D_STATE:(g + 1) * D_STATE], bm[:, g * D_STATE:(g + 1) * D_STATE])
          for g in range(B_GROUPS)]

    P = B_HEAD_DIM
    low = _lane_iota((L, 2 * P)) < P
    top = lax.broadcasted_iota(jnp.int32, (2 * P, 1), 0) < P
    ys = []
    for j in range(B_HEADS // 2):
        h0, h1 = 2 * j, 2 * j + 1
        g = h0 // B_REP
        cm_g = cm[:, g * D_STATE:(g + 1) * D_STATE]
        bm_g = bm[:, g * D_STATE:(g + 1) * D_STATE]
        cumb = [jnp.broadcast_to(cum[:, h:h + 1], (L, 2 * P)) for h in (h0, h1)]
        cum_end = [cum[L - 1:L, h:h + 1] for h in (h0, h1)]
        xs_pair = xs[:, j * 2 * P:(j + 1) * 2 * P]
        xdt = xs_pair * jnp.where(low, dt[:, h0:h0 + 1], dt[:, h1:h1 + 1])
        y = _dot_nt(cm_g, jnp.concatenate([state_ref[0, h0], state_ref[0, h1]], axis=0).astype(BF16))
        y = y * jnp.exp(jnp.where(low, cumb[0], cumb[1]))
        for e, h in enumerate((h0, h1)):
            decay = jnp.exp(jnp.where(causal, cumb[e] - cum_t[h:h + 1, :], -jnp.inf))
            x_h = jnp.where(low if e == 0 else ~low, xdt, 0.0)
            y = y + _dot((cb[g] * decay).astype(BF16), x_h.astype(BF16))
        d_pair = jnp.where(low[0:1], dskip_ref[h0], dskip_ref[h1])
        ys.append(y + d_pair * xs_pair)
        xw = xdt * jnp.exp(jnp.where(low, cum_end[0] - cumb[0], cum_end[1] - cumb[1]))
        new = _dot_tn(xw.astype(BF16), bm_g)
        keep = jnp.where(top, jnp.exp(cum_end[0]), jnp.exp(cum_end[1]))
        state_ref[0, h0] = state_ref[0, h0] * keep[:P] + new[:P]
        state_ref[0, h1] = state_ref[0, h1] * keep[P:] + new[P:]
    y = jnp.concatenate(ys, axis=1)
    o_ref[...] = _gated_rmsnorm(y, z_ref[...].astype(F32), nw_ref[...]).astype(o_ref.dtype)


def _ssd_prompt(xbc, dt, z, conv_w, conv_b, dt_bias, a_log, d_skip, norm_w, batch, seq):
    nc = seq // SSD_CHUNK
    cur = lambda b, c: (b * nc + c, 0)
    return pl.pallas_call(
        _ssd_prompt_kernel,
        grid=(batch, nc),
        in_specs=[pl.BlockSpec(memory_space=pltpu.SMEM),
                  pl.BlockSpec((SSD_CHUNK, CONV_DIM), cur),
                  pl.BlockSpec((SSD_CHUNK, LANES), cur),
                  pl.BlockSpec((SSD_CHUNK, B_WIDTH), cur),
                  _const_spec(conv_w.shape), _const_spec(conv_b.shape), _const_spec(dt_bias.shape),
                  _const_spec(a_log.shape), _const_spec(norm_w.shape)],
        out_specs=[pl.BlockSpec((SSD_CHUNK, B_WIDTH), cur),
                   pl.BlockSpec((1, B_HEADS, B_HEAD_DIM, D_STATE), lambda b, c: (b, 0, 0, 0))],
        out_shape=[jax.ShapeDtypeStruct((batch * seq, B_WIDTH), BF16),
                   jax.ShapeDtypeStruct((batch, B_HEADS, B_HEAD_DIM, D_STATE), F32)],
        scratch_shapes=[pltpu.VMEM((8 + SSD_CHUNK, CONV_DIM), F32)],
        compiler_params=_cparams("parallel", "arbitrary"),
        name="ssd_prompt",
    )(d_skip, xbc, dt, z, conv_w, conv_b, dt_bias, a_log, norm_w)


def _ssd_sample_kernel(dskip_ref, xbc_ref, hist_ref, dt_ref, z_ref, h0_ref, cw_ref, cb_ref, dtb_ref,
                       alog_ref, nw_ref, o_ref, conv_o, state_o):
    x = xbc_ref[...]
    hist = hist_ref[...]
    acc = cb_ref[...][None] + x * cw_ref[CONV_WIDTH - 1:CONV_WIDTH, :][None]
    for j in range(CONV_WIDTH - 1):
        acc = acc + hist[:, j:j + 1, :] * cw_ref[j:j + 1, :][None]
    conv_o[:, 0:CONV_WIDTH - 2, :] = hist[:, 1:CONV_WIDTH - 1, :]
    conv_o[:, CONV_WIDTH - 2:CONV_WIDTH - 1, :] = x
    u = _silu(acc)
    xs = u[:, :, :B_WIDTH]
    bm = u[:, :, B_WIDTH:B_WIDTH + B_GROUPS * D_STATE]
    cm = u[:, :, B_WIDTH + B_GROUPS * D_STATE:]
    dt = _softplus(dt_ref[...] + dtb_ref[...][None])
    dec = jnp.exp(dt * (-jnp.exp(alog_ref[...]))[None])
    ys = []
    for h in range(B_HEADS):
        g = h // B_REP
        xs_h = xs[:, :, h * B_HEAD_DIM:(h + 1) * B_HEAD_DIM]
        xdt = xs_h * dt[:, :, h:h + 1]
        bm_g = bm[:, :, g * D_STATE:(g + 1) * D_STATE]
        cm_g = cm[:, :, g * D_STATE:(g + 1) * D_STATE]
        dec_h = dec[:, :, h:h + 1]
        st = h0_ref[:, h]
        cb = jnp.sum(cm_g * bm_g, axis=-1, keepdims=True)
        y_off = jnp.einsum("bqn,bpn->bqp", cm_g.astype(BF16), st.astype(BF16),
                           preferred_element_type=F32)
        ys.append(cb * xdt + y_off * dec_h + dskip_ref[h] * xs_h)
        outer = jnp.einsum("bqp,bqn->bpn", xdt.astype(BF16), bm_g.astype(BF16),
                           preferred_element_type=F32)
        state_o[:, h] = st * dec_h + outer
    y = jnp.concatenate(ys, axis=-1)
    o_ref[...] = _gated_rmsnorm(y, z_ref[...], nw_ref[...][None]).astype(o_ref.dtype)


def _ssd_sample(xbc3, hist, dt3, z3, h0, conv_w, conv_b, dt_bias, a_log, d_skip, norm_w, bb):
    nb = xbc3.shape[0]
    b3 = lambda i: (i, 0, 0)
    b4 = lambda i: (i, 0, 0, 0)
    return pl.pallas_call(
        _ssd_sample_kernel,
        grid=(nb // bb,),
        in_specs=[pl.BlockSpec(memory_space=pltpu.SMEM),
                  pl.BlockSpec((bb, 1, CONV_DIM), b3),
                  pl.BlockSpec((bb, CONV_WIDTH - 1, CONV_DIM), b3),
                  pl.BlockSpec((bb, 1, LANES), b3),
                  pl.BlockSpec((bb, 1, B_WIDTH), b3),
                  pl.BlockSpec((bb, B_HEADS, B_HEAD_DIM, D_STATE), b4),
                  _const_spec(conv_w.shape), _const_spec(conv_b.shape), _const_spec(dt_bias.shape),
                  _const_spec(a_log.shape), _const_spec(norm_w.shape)],
        out_specs=[pl.BlockSpec((bb, 1, B_WIDTH), b3),
                   pl.BlockSpec((bb, CONV_WIDTH - 1, CONV_DIM), b3),
                   pl.BlockSpec((bb, B_HEADS, B_HEAD_DIM, D_STATE), b4)],
        out_shape=[jax.ShapeDtypeStruct((nb, 1, B_WIDTH), F32),
                   jax.ShapeDtypeStruct((nb, CONV_WIDTH - 1, CONV_DIM), F32),
                   jax.ShapeDtypeStruct((nb, B_HEADS, B_HEAD_DIM, D_STATE), F32)],
        compiler_params=_cparams("parallel"),
        name="ssd_sample",
    )(d_skip, xbc3, hist, dt3, z3, h0, conv_w, conv_b, dt_bias, a_log, norm_w)


def _outproj_ln_kernel(n_parts, x_ref, *refs):
    parts, ws = refs[:n_parts], refs[n_parts:2 * n_parts]
    g_ref, b_ref, o_ref = refs[2 * n_parts:]
    mix = _dot(parts[0][...].astype(BF16), ws[0][...])
    for a, w in zip(parts[1:], ws[1:]):
        mix = mix + _dot(a[...].astype(BF16), w[...])
    y = DN_ALPHA * x_ref[...] + mix
    mu = jnp.mean(y, axis=-1, keepdims=True)
    var = jnp.mean(jnp.square(y - mu), axis=-1, keepdims=True)
    o_ref[...] = (y - mu) * lax.rsqrt(var + LN_EPS) * g_ref[...] + b_ref[...]


def _outproj_ln(x2d, parts, ws, ln_g, ln_b, tm):
    n = x2d.shape[0]
    row = lambda i: (i, 0)
    return pl.pallas_call(
        functools.partial(_outproj_ln_kernel, len(parts)),
        grid=(n // tm,),
        in_specs=[pl.BlockSpec((tm, D_MODEL), row)]
                 + [pl.BlockSpec((tm, p.shape[1]), row) for p in parts]
                 + [_const_spec(w.shape) for w in ws]
                 + [_const_spec(ln_g.shape), _const_spec(ln_b.shape)],
        out_specs=pl.BlockSpec((tm, D_MODEL), row),
        out_shape=jax.ShapeDtypeStruct((n, D_MODEL), F32),
        compiler_params=_cparams("parallel"),
        name="outproj_ln",
    )(x2d, *parts, *ws, ln_g, ln_b)


def _bf16_pieces(c):
    pieces, rem = [], c
    for _ in range(C_PIECES):
        p = rem.astype(BF16).astype(F32)
        pieces.append(p)
        rem = rem - p
    return pieces


def _odd_inproj_prompt_kernel(tiles_per_seq, x_ref, wq, wk, wv, wf, wg, fb_ref, sq_ref, sk_ref, cq_ref, ck_ref,
                              qa_o, ka_o, vt_o, kt_o, vtf_o, lf_o, g_o, carry_ref):
    i = pl.program_id(0)
    tm = x_ref.shape[0]

    @pl.when(i % tiles_per_seq == 0)
    def _():
        carry_ref[...] = jnp.zeros_like(carry_ref)

    xb = x_ref[...].astype(BF16)
    kf = _dot(xb, wk[...])
    vf = _dot(xb, wv[...])
    g_o[...] = _dot(xb, wg[...]).astype(g_o.dtype)
    lf = _log_sigmoid(_dot(xb, wf[...]) + fb_ref[...])
    lf_o[...] = lf[:, :C_HEADS]

    c = jnp.where(_lane_iota(lf.shape) < C_HEADS, lf, 0.0)
    rows = lax.broadcasted_iota(jnp.int32, c.shape, 0)
    s = 1
    while s < tm:
        c = c + jnp.where(rows >= s, pltpu.roll(c, s, axis=0), 0.0)
        s *= 2
    c = c + carry_ref[...]
    carry_ref[...] = c[tm - 1:tm, :]

    hi, mid, lo = _bf16_pieces(c * LOG2E)
    c3 = (hi + pltpu.roll(mid, C_HEADS, axis=1) + pltpu.roll(lo, 2 * C_HEADS, axis=1)).astype(BF16)
    ex_q = _dot(c3, sq_ref[...]) + cq_ref[...]
    ex_k = _dot(c3, sk_ref[...]) + ck_ref[...]

    qf = _dot(xb, wq[...])
    for h, t in enumerate(_split_heads_to_lane_tiles(qf, C_WIDTH // LANES)):
        sl = slice(h * LANES, (h + 1) * LANES)
        qa_o[:, sl] = (t + ex_q[:, sl]).astype(qa_o.dtype)
    ones_row = (_lane_iota((1, LANES)) == HEAD_DIM).astype(F32)
    k_tiles = _split_heads_to_lane_tiles(kf, C_KV_WIDTH // LANES)
    v_tiles = _split_heads_to_lane_tiles(vf, C_KV_WIDTH // LANES)
    for g in range(C_KV_HEADS):
        sl = slice(g * LANES, (g + 1) * LANES)
        ka_o[:, sl] = (k_tiles[g] + ex_k[:, sl]).astype(ka_o.dtype)
        v_t = (v_tiles[g] + ones_row).T
        vt_o[0, g] = v_t.astype(vt_o.dtype)
        vtf_o[0, g] = v_t[:HEAD_DIM]
        kt_o[0, g] = k_tiles[g].T[:HEAD_DIM]


def _aug_selectors():
    sq = np.zeros((LANES, C_HEADS * LANES), np.float32)
    sk = np.zeros((LANES, C_KV_HEADS * LANES), np.float32)
    cq = np.zeros((1, C_HEADS * LANES), np.float32)
    ck = np.zeros((1, C_KV_HEADS * LANES), np.float32)
    for h in range(C_HEADS):
        g, r = divmod(h, C_REP)
        for p in range(C_PIECES):
            sq[p * C_HEADS + h, h * LANES + HEAD_DIM + p] = 1.0
            col = HEAD_DIM + C_PIECES + C_PIECES * r + p
            cq[0, h * LANES + col] = -1.0
            sk[p * C_HEADS + h, g * LANES + col] = 1.0
    for g in range(C_KV_HEADS):
        for p in range(C_PIECES):
            ck[0, g * LANES + HEAD_DIM + p] = 1.0
    return jnp.asarray(sq, BF16), jnp.asarray(sk, BF16), jnp.asarray(cq), jnp.asarray(ck)


def _odd_inproj_prompt(x2d, w, f_bias, batch, seq, tm):
    n = x2d.shape[0]
    tiles = seq // tm
    sq, sk, cq, ck = _aug_selectors()
    row = lambda i: (i, 0)
    consts = list(w) + [f_bias, sq, sk, cq, ck]
    outs = [((n, C_HEADS * LANES), BF16), ((n, C_KV_HEADS * LANES), BF16),
            ((batch, C_KV_HEADS, LANES, seq), BF16), ((batch, C_KV_HEADS, HEAD_DIM, seq), F32),
            ((batch, C_KV_HEADS, HEAD_DIM, seq), F32), ((n, C_HEADS), F32), ((n, C_WIDTH), BF16)]
    seq_minor = lambda i: (i // tiles, 0, 0, i % tiles)
    out_specs = [pl.BlockSpec((tm, outs[0][0][1]), row), pl.BlockSpec((tm, outs[1][0][1]), row),
                 pl.BlockSpec((1, C_KV_HEADS, LANES, tm), seq_minor),
                 pl.BlockSpec((1, C_KV_HEADS, HEAD_DIM, tm), seq_minor),
                 pl.BlockSpec((1, C_KV_HEADS, HEAD_DIM, tm), seq_minor),
                 pl.BlockSpec((tm, C_HEADS), row), pl.BlockSpec((tm, C_WIDTH), row)]
    return pl.pallas_call(
        functools.partial(_odd_inproj_prompt_kernel, tiles),
        grid=(n // tm,),
        in_specs=[pl.BlockSpec((tm, D_MODEL), row)] + [_const_spec(a.shape) for a in consts],
        out_specs=out_specs,
        out_shape=[jax.ShapeDtypeStruct(s, d) for s, d in outs],
        scratch_shapes=[pltpu.VMEM((1, LANES), F32)],
        compiler_params=_cparams("arbitrary"),
        name="odd_inproj_prompt",
    )(x2d, *consts)


def _odd_inproj_sample_kernel(x_ref, wq, wk, wv, wf, wg, fb_ref, q_o, k_o, v_o, lf_o, g_o):
    xb = x_ref[...].astype(BF16)
    q_o[...] = _dot(xb, wq[...])
    k_o[...] = _dot(xb, wk[...])
    v_o[...] = _dot(xb, wv[...])
    lf_o[...] = _log_sigmoid(_dot(xb, wf[...]) + fb_ref[...])[:, :C_HEADS]
    g_o[...] = _dot(xb, wg[...])


def _odd_inproj_sample(x2d, w, f_bias):
    n = x2d.shape[0]
    outs = [((n, C_WIDTH), F32), ((n, C_KV_WIDTH), F32), ((n, C_KV_WIDTH), F32), ((n, C_HEADS), F32),
            ((n, C_WIDTH), F32)]
    return pl.pallas_call(
        _odd_inproj_sample_kernel,
        out_shape=[jax.ShapeDtypeStruct(s, d) for s, d in outs],
        compiler_params=pltpu.CompilerParams(vmem_limit_bytes=VMEM_LIMIT),
        name="odd_inproj_sample",
    )(x2d, *w, f_bias)


def _fox_prompt_kernel(q_ref, k_ref, vt_ref, g_ref, o_ref, qs_ref, m_ref, alpha_ref, acc_ref, p_ref, s_ref):
    qi = pl.program_id(1)
    tq, tk = FOX_TQ, FOX_TK
    for g in range(C_KV_HEADS):
        for r in range(C_REP):
            h = g * C_REP + r
            qs_ref[g, r * tq:(r + 1) * tq, :] = q_ref[:, h * LANES:(h + 1) * LANES]
    m_ref[...] = jnp.full(m_ref.shape, -jnp.inf, F32)
    acc_ref[...] = jnp.zeros_like(acc_ref)
    n_last = (qi * tq) // tk

    def qk(t, slot, masked):
        start = t * tk if isinstance(t, int) else pl.multiple_of(t * tk, tk)
        for g in range(C_KV_HEADS):
            k_t = k_ref[pl.ds(start, tk), g * LANES:(g + 1) * LANES]
            s_t = _dot_nt(k_t, qs_ref[g])
            if masked:
                krow = lax.broadcasted_iota(jnp.int32, (tk, 1), 0)
                qrel = (qi * tq - start) + _lane_iota((1, C_REP * tq)) % tq
                s_t = jnp.where(krow <= qrel, s_t, -jnp.inf)
            s_ref[slot, g] = s_t

    def softmax(slot):
        for g in range(C_KV_HEADS):
            s_t = s_ref[slot, g]
            m_old = m_ref[g]
            m_new = jnp.maximum(m_old, jnp.max(s_t, axis=0, keepdims=True))
            p_ref[slot, g] = jnp.exp2(s_t - m_new).astype(BF16)
            alpha_ref[slot, g] = jnp.exp2(m_old - m_new)
            m_ref[g] = m_new

    def pv(t, slot):
        start = t * tk if isinstance(t, int) else pl.multiple_of(t * tk, tk)
        for g in range(C_KV_HEADS):
            acc_ref[g] = (acc_ref[g] * alpha_ref[slot, g]
                          + _dot(vt_ref[0, g, :, pl.ds(start, tk)], p_ref[slot, g]))

    def guarded_step(tau, parity):
        @pl.when(tau <= n_last)
        def _():
            qk(tau, parity, True)

        @pl.when((tau >= 1) & (tau <= n_last + 1))
        def _():
            softmax(1 - parity)

        @pl.when((tau >= 2) & (tau <= n_last + 2))
        def _():
            pv(tau - 2, parity)

    def steady_pair(i, carry):
        tau = 2 + 2 * i
        qk(tau, 0, False)
        softmax(1)
        pv(tau - 2, 0)
        qk(tau + 1, 1, False)
        softmax(0)
        pv(tau - 1, 1)
        return carry

    @pl.when(n_last < 2)
    def _():
        for tau in range(4):
            guarded_step(tau, tau % 2)

    @pl.when(n_last >= 2)
    def _():
        qk(0, 0, False)
        qk(1, 1, False)
        softmax(0)
        n_pairs = (n_last - 2) // 2
        lax.fori_loop(0, n_pairs, steady_pair, 0)

        @pl.when(n_last % 2 == 0)
        def _():
            qk(n_last, 0, True)
            softmax(1)
            pv(n_last - 2, 0)
            softmax(0)
            pv(n_last - 1, 1)
            pv(n_last, 0)

        @pl.when(n_last % 2 == 1)
        def _():
            qk(n_last - 1, 0, False)
            softmax(1)
            pv(n_last - 3, 0)
            qk(n_last, 1, True)
            softmax(0)
            pv(n_last - 2, 1)
            softmax(1)
            pv(n_last - 1, 0)
            pv(n_last, 1)

    tiles = []
    for g in range(C_KV_HEADS):
        acc = acc_ref[g]
        o_t = acc[:HEAD_DIM] / acc[HEAD_DIM:HEAD_DIM + 1]
        tiles.extend(o_t[:, r * tq:(r + 1) * tq].T for r in range(C_REP))
    o = jnp.concatenate(tiles, axis=1)
    o_ref[...] = (o * _silu(g_ref[...].astype(F32))).astype(o_ref.dtype)


def _fox_prompt(q_aug, k_aug, vt_aug, g, batch, seq):
    nq = seq // FOX_TQ
    row = lambda b, i: (b * nq + i, 0)
    once = pl.Buffered(1)
    return pl.pallas_call(
        _fox_prompt_kernel,
        grid=(batch, nq),
        in_specs=[pl.BlockSpec((FOX_TQ, C_HEADS * LANES), row),
                  pl.BlockSpec((seq, C_KV_HEADS * LANES), lambda b, i: (b, 0), pipeline_mode=once),
                  pl.BlockSpec((1, C_KV_HEADS, LANES, seq), lambda b, i: (b, 0, 0, 0), pipeline_mode=once),
                  pl.BlockSpec((FOX_TQ, C_WIDTH), row)],
        out_specs=pl.BlockSpec((FOX_TQ, C_WIDTH), row),
        out_shape=jax.ShapeDtypeStruct((batch * seq, C_WIDTH), BF16),
        scratch_shapes=[pltpu.VMEM((C_KV_HEADS, C_REP * FOX_TQ, LANES), BF16),
                        pltpu.VMEM((C_KV_HEADS, 1, C_REP * FOX_TQ), F32),
                        pltpu.VMEM((2, C_KV_HEADS, 1, C_REP * FOX_TQ), F32),
                        pltpu.VMEM((C_KV_HEADS, LANES, C_REP * FOX_TQ), F32),
                        pltpu.VMEM((2, C_KV_HEADS, FOX_TK, C_REP * FOX_TQ), BF16),
                        pltpu.VMEM((2, C_KV_HEADS, FOX_TK, C_REP * FOX_TQ), F32)],
        compiler_params=_cparams("parallel", "arbitrary"),
        name="fox_prompt",
    )(q_aug, k_aug, vt_aug, g)


def _fox_decode_kernel(pt_ref, q_ref, kn_ref, vn_ref, lfn_ref, g_ref, k_hbm, v_hbm, lf_hbm, o_ref,
                       kbuf, vbuf, lfbuf, sem, qm_ref, m_ref, l_ref, acc_ref, carry_ref):
    b, c = pl.program_id(0), pl.program_id(1)
    nb, nc = pl.num_programs(0), pl.num_programs(1)
    G = DEC_PAGES
    n_pages = nc * G
    step = b * nc + c
    slot = step % 2

    def copies(bb, cc, sl):
        out = []
        for g in range(G):
            page = pt_ref[bb, n_pages - (cc + 1) * G + g]
            out.append((pltpu.make_async_copy(k_hbm.at[page], kbuf.at[sl, g], sem.at[0, sl]), g % 2))
            out.append((pltpu.make_async_copy(v_hbm.at[page], vbuf.at[sl, g], sem.at[1, sl]), (g + 1) % 2))
            out.append((pltpu.make_async_copy(lf_hbm.at[page], lfbuf.at[sl, g], sem.at[2, sl]), 0))
        return out

    @pl.when(step == 0)
    def _():
        for cp, prio in copies(0, 0, 0):
            cp.start(priority=prio)

    @pl.when(step + 1 < nb * nc)
    def _():
        nxt = step + 1
        for cp, prio in copies(nxt // nc, nxt % nc, 1 - slot):
            cp.start(priority=prio)

    @pl.when(c == 0)
    def _():
        q16 = q_ref[0]
        q4 = jnp.concatenate([q16] * C_KV_HEADS, axis=1)
        hrow = lax.broadcasted_iota(jnp.int32, q4.shape, 0) // C_REP
        gcol = _lane_iota(q4.shape) // HEAD_DIM
        qm = jnp.where(hrow == gcol, q4, 0.0).astype(BF16)
        qm_ref[...] = qm
        kn = kn_ref[0].astype(BF16).astype(F32)
        m_ref[...] = jnp.sum(qm.astype(F32) * kn, axis=-1, keepdims=True)
        l_ref[...] = jnp.ones_like(l_ref)
        acc_ref[...] = jnp.broadcast_to(vn_ref[0].astype(BF16).astype(F32), acc_ref.shape)
        carry_ref[...] = lfn_ref[0]

    for cp, _ in copies(b, c, slot):
        cp.wait()

    lf_rows = lfbuf[slot].reshape(G * C_HEADS, PAGE_SIZE)
    hi = lf_rows.astype(BF16)
    lo = (lf_rows - hi.astype(F32)).astype(BF16)
    ii = lax.broadcasted_iota(jnp.int32, (PAGE_SIZE, PAGE_SIZE), 0)
    jj = lax.broadcasted_iota(jnp.int32, (PAGE_SIZE, PAGE_SIZE), 1)
    later = (ii > jj).astype(BF16)
    within = _dot(hi, later) + _dot(lo, later)
    carry = carry_ref[...]
    cols = [None] * G
    for g in reversed(range(G)):
        rs = slice(g * C_HEADS, (g + 1) * C_HEADS)
        cols[g] = within[rs] + carry
        carry = carry + within[rs][:, 0:1] + lf_rows[rs][:, 0:1]
    carry_ref[...] = carry

    kc = jnp.concatenate([kbuf[slot, g].astype(BF16) for g in range(G)], axis=1)
    s = _dot(qm_ref[...], kc) + jnp.concatenate(cols, axis=1)
    m_old = m_ref[...]
    m_new = jnp.maximum(m_old, jnp.max(s, axis=-1, keepdims=True))
    alpha = jnp.exp(m_old - m_new)
    p = jnp.exp(s - m_new)
    l_ref[...] = l_ref[...] * alpha + jnp.sum(p, axis=-1, keepdims=True)
    vc = jnp.concatenate([vbuf[slot, g].astype(BF16) for g in range(G)], axis=1)
    acc_ref[...] = acc_ref[...] * alpha + _dot_nt(p.astype(BF16), vc)
    m_ref[...] = m_new

    @pl.when(c == nc - 1)
    def _():
        o_all = acc_ref[...] / l_ref[...]
        hrow = lax.broadcasted_iota(jnp.int32, (C_HEADS, HEAD_DIM), 0) // C_REP
        o = jnp.zeros((C_HEADS, HEAD_DIM), F32)
        for g in range(C_KV_HEADS):
            o = jnp.where(hrow == g, o_all[:, g * HEAD_DIM:(g + 1) * HEAD_DIM], o)
        o_ref[0] = (o * _silu(g_ref[0])).astype(o_ref.dtype)


def _fox_decode(page_table, q3, k_new, v_new, lf_new, g3, cache_k, cache_v, cache_lf):
    nb, n_pages = page_table.shape
    nc = n_pages // DEC_PAGES
    b3 = lambda b, c, pt: (b, 0, 0)
    grid_spec = pltpu.PrefetchScalarGridSpec(
        num_scalar_prefetch=1,
        grid=(nb, nc),
        in_specs=[pl.BlockSpec((1, C_HEADS, HEAD_DIM), b3),
                  pl.BlockSpec((1, 1, C_KV_WIDTH), b3),
                  pl.BlockSpec((1, 1, C_KV_WIDTH), b3),
                  pl.BlockSpec((1, C_HEADS, 1), b3),
                  pl.BlockSpec((1, C_HEADS, HEAD_DIM), b3),
                  pl.BlockSpec(memory_space=pl.ANY),
                  pl.BlockSpec(memory_space=pl.ANY),
                  pl.BlockSpec(memory_space=pl.ANY)],
        out_specs=pl.BlockSpec((1, C_HEADS, HEAD_DIM), b3),
        scratch_shapes=[pltpu.VMEM((2, DEC_PAGES, C_KV_WIDTH, PAGE_SIZE), F32),
                        pltpu.VMEM((2, DEC_PAGES, C_KV_WIDTH, PAGE_SIZE), F32),
                        pltpu.VMEM((2, DEC_PAGES, C_HEADS, PAGE_SIZE), F32),
                        pltpu.SemaphoreType.DMA((3, 2)),
                        pltpu.VMEM((C_HEADS, C_KV_WIDTH), BF16),
                        pltpu.VMEM((C_HEADS, 1), F32),
                        pltpu.VMEM((C_HEADS, 1), F32),
                        pltpu.VMEM((C_HEADS, C_KV_WIDTH), F32),
                        pltpu.VMEM((C_HEADS, 1), F32)])
    return pl.pallas_call(
        _fox_decode_kernel,
        grid_spec=grid_spec,
        out_shape=jax.ShapeDtypeStruct((nb, C_HEADS, HEAD_DIM), BF16),
        compiler_params=_cparams("arbitrary", "arbitrary"),
        name="fox_decode",
    )(page_table, q3, k_new, v_new, lf_new, g3, cache_k, cache_v, cache_lf)


def _split_cols(w, sizes):
    offs = np.cumsum(sizes)[:-1].tolist()
    return jnp.split(w, offs, axis=-1)


def _pad_lanes(a):
    return jnp.pad(a, ((0, 0), (0, LANES - a.shape[-1])))


def _rope_tables(pos):
    half = HEAD_DIM // 2
    inv_freq = ROPE_THETA ** (-jnp.arange(half, dtype=F32) / half)
    ang = pos.astype(F32)[:, None] * inv_freq[None, :]
    cos, sin = jnp.cos(ang), jnp.sin(ang)
    reps = LANES // HEAD_DIM
    return (jnp.tile(jnp.concatenate([cos, cos], axis=1), (1, reps)),
            jnp.tile(jnp.concatenate([-sin, sin], axis=1), (1, reps)))


def _even_weights(w_in, conv_w, conv_b, dt_bias, a_log, ssm_norm_w, w_out, ln_g, ln_b):
    wq, wk, wv, wg, wz, wx, wdt = _split_cols(w_in, EVEN_SPLITS)
    proj = [a.astype(BF16) for a in (wq, wk, wv, wg, wz, wx, _pad_lanes(wdt))]
    wo = w_out.astype(BF16)
    return dict(proj=proj, conv_w=conv_w, conv_b=conv_b[None], dt_bias=_pad_lanes(dt_bias[None]),
                a_log=_pad_lanes(a_log[None]), norm_w=ssm_norm_w[None],
                wo=(wo[:A_WIDTH], wo[A_WIDTH:]), ln_g=ln_g[None], ln_b=ln_b[None])


def _odd_weights(w_in, f_bias, w_out, ln_g, ln_b):
    wq, wk, wv, wf, wg = _split_cols(w_in, ODD_SPLITS)
    proj = [a.astype(BF16) for a in (wq * ATTN_SCALE, wk, wv, _pad_lanes(wf), wg)]
    proj_log2 = [(wq * (ATTN_SCALE * LOG2E)).astype(BF16)] + proj[1:]
    return dict(proj=proj, proj_log2=proj_log2, f_bias=_pad_lanes(f_bias[None]), wo=(w_out.astype(BF16),),
                ln_g=ln_g[None], ln_b=ln_b[None])


def _even_prompt(x, ew, sinks, d_skip):
    b, t, _ = x.shape
    x2 = x.reshape(b * t, D_MODEL)
    cos, sin = _rope_tables(jnp.arange(t, dtype=jnp.int32))
    q, k, v, g, z, xbc, dt = _even_inproj(x2, cos, sin, ew["proj"], tm=1024, act_dtype=BF16)
    o_a = _swa_prompt(q, k, v, g, sinks, b, t)
    o_b, state = _ssd_prompt(xbc, dt, z, ew["conv_w"], ew["conv_b"], ew["dt_bias"], ew["a_log"], d_skip,
                             ew["norm_w"], b, t)
    y = _outproj_ln(x2, (o_a, o_b), ew["wo"], ew["ln_g"], ew["ln_b"], tm=512)
    new_k = k.reshape(b, t, A_KV_WIDTH)[:, -WINDOW:].reshape(b, WINDOW, A_KV_HEADS, HEAD_DIM)
    new_v = v.reshape(b, t, A_KV_WIDTH)[:, -WINDOW:].reshape(b, WINDOW, A_KV_HEADS, HEAD_DIM)
    new_conv = xbc.reshape(b, t, CONV_DIM)[:, -(CONV_WIDTH - 1):]
    return y.reshape(b, t, D_MODEL), new_k, new_v, new_conv, state


def _even_sample(x, pos, ew, sinks, d_skip, swa_k, swa_v, conv_hist, ssm_h0):
    b, t, _ = x.shape
    x2 = x.reshape(b, D_MODEL)
    cos, sin = _rope_tables(jnp.full((b,), pos, jnp.int32))
    q, k, v, g, z, xbc, dt = _even_inproj(x2, cos, sin, ew["proj"], tm=b, act_dtype=F32)
    o_a, new_k, new_v = _swa_sample(
        q.reshape(b, A_HEADS, HEAD_DIM), k.reshape(b, 1, A_KV_WIDTH), v.reshape(b, 1, A_KV_WIDTH),
        swa_k.reshape(b, WINDOW, A_KV_WIDTH), swa_v.reshape(b, WINDOW, A_KV_WIDTH),
        g.reshape(b, A_HEADS, HEAD_DIM), sinks, bb=16)
    o_b, new_conv, new_state = _ssd_sample(xbc.reshape(b, 1, CONV_DIM), conv_hist, dt.reshape(b, 1, LANES),
                                           z.reshape(b, 1, B_WIDTH), ssm_h0,
                                           ew["conv_w"], ew["conv_b"], ew["dt_bias"], ew["a_log"], d_skip,
                                           ew["norm_w"], bb=8)
    y = _outproj_ln(x2, (o_a.reshape(b, A_WIDTH), o_b.reshape(b, B_WIDTH)), ew["wo"], ew["ln_g"], ew["ln_b"],
                    tm=b)
    return (y.reshape(b, 1, D_MODEL), new_k.reshape(b, WINDOW, A_KV_HEADS, HEAD_DIM),
            new_v.reshape(b, WINDOW, A_KV_HEADS, HEAD_DIM), new_conv, new_state)


def _odd_prompt(x, ow):
    b, t, _ = x.shape
    x2 = x.reshape(b * t, D_MODEL)
    q_aug, k_aug, vt_aug, kt, vt, lf, g = _odd_inproj_prompt(x2, ow["proj_log2"], ow["f_bias"], b, t, tm=512)
    o = _fox_prompt(q_aug, k_aug, vt_aug, g, b, t)
    y = _outproj_ln(x2, (o,), ow["wo"], ow["ln_g"], ow["ln_b"], tm=512)
    return (y.reshape(b, t, D_MODEL), jnp.transpose(kt, (0, 3, 1, 2)), jnp.transpose(vt, (0, 3, 1, 2)),
            lf.reshape(b, t, C_HEADS))


def _odd_sample(x, ow, cache_k, cache_v, cache_lf, page_table):
    b, t, _ = x.shape
    x2 = x.reshape(b, D_MODEL)
    q, k, v, lf, g = _odd_inproj_sample(x2, ow["proj"], ow["f_bias"])
    n_phys = cache_k.shape[0]
    kt = jnp.transpose(cache_k, (0, 2, 3, 1)).reshape(n_phys, C_KV_WIDTH, PAGE_SIZE)
    vt = jnp.transpose(cache_v, (0, 2, 3, 1)).reshape(n_phys, C_KV_WIDTH, PAGE_SIZE)
    lft = jnp.swapaxes(cache_lf, 1, 2)
    o = _fox_decode(page_table, q.reshape(b, C_HEADS, HEAD_DIM), k.reshape(b, 1, C_KV_WIDTH),
                    v.reshape(b, 1, C_KV_WIDTH), lf.reshape(b, C_HEADS, 1), g.reshape(b, C_HEADS, HEAD_DIM),
                    kt, vt, lft)
    y = _outproj_ln(x2, (o.reshape(b, C_WIDTH),), ow["wo"], ow["ln_g"], ow["ln_b"], tm=b)
    return (y.reshape(b, 1, D_MODEL), k.reshape(b, 1, C_KV_HEADS, HEAD_DIM),
            v.reshape(b, 1, C_KV_HEADS, HEAD_DIM), lf.reshape(b, 1, C_HEADS))


def kernel(x_prompt, x_sample, cache_swa_k, cache_swa_v, state_conv, state_ssm, cache_fox_k, cache_fox_v, cache_fox_logf, page_table, w_in_even, attn_sinks, conv_w, conv_b, dt_bias, a_log, d_skip, ssm_norm_w, w_out_even, ln_g_even, ln_b_even, w_in_odd, forget_bias, w_out_odd, ln_g_odd, ln_b_odd):
    past_len = page_table.shape[1] * PAGE_SIZE
    ew = _even_weights(w_in_even[0], conv_w[0], conv_b[0], dt_bias[0], a_log[0], ssm_norm_w[0],
                       w_out_even[0], ln_g_even[0], ln_b_even[0])
    ow = _odd_weights(w_in_odd[0], forget_bias[0], w_out_odd[0], ln_g_odd[0], ln_b_odd[0])

    yp, swa_kp, swa_vp, conv_p, ssm_p = _even_prompt(x_prompt, ew, attn_sinks[0], d_skip[0])
    ys, swa_ks, swa_vs, conv_s, ssm_s = _even_sample(x_sample, past_len, ew, attn_sinks[0], d_skip[0],
                                                     cache_swa_k[0], cache_swa_v[0], state_conv[0], state_ssm[0])
    yp, fox_kp, fox_vp, fox_lp = _odd_prompt(yp, ow)
    ys, fox_ks, fox_vs, fox_ls = _odd_sample(ys, ow, cache_fox_k[0], cache_fox_v[0], cache_fox_logf[0], page_table)

    one = lambda a: a[None]
    return (yp, ys, one(swa_kp), one(swa_vp), one(swa_ks), one(swa_vs), one(conv_p), one(conv_s),
            one(ssm_p), one(ssm_s), one(fox_kp), one(fox_vp), one(fox_lp), one(fox_ks), one(fox_vs), one(fox_ls))
```
